```python
import math
import jax, jax.numpy as jnp
from jax import lax
import numpy as np

D_MODEL = 2048
BATCH = 2
SEQ = 4096
DEPTH = 2

MEM_LEN = 256
CONV_WIDTH = 1024
CONV_K = 31
SSM_WIDTH = 1024
SSM_GROUP = 16
SSM_GROUPS = SSM_WIDTH // SSM_GROUP
SSM_STATE = 64
XA_HEADS = 4
XA_HEAD_DIM = D_MODEL // XA_HEADS
D_FF = 5632
FFN_K = 3
EPS = 1e-6
DT_MIN = 1e-3
DT_MAX = 1e-1
IN_COLS = 2 * CONV_WIDTH + SSM_WIDTH + 2 * D_MODEL

kernel_name = "griffin_gated_conformer_s5_hybrid"


def rmsnorm(x, g):
    xf = x.astype(jnp.float32)
    y = xf * lax.rsqrt(jnp.mean(xf * xf, axis=-1, keepdims=True) + EPS)
    return (y * g.astype(jnp.float32)).astype(x.dtype)


def layernorm(x, g, b):
    xf = x.astype(jnp.float32)
    mu = jnp.mean(xf, axis=-1, keepdims=True)
    xc = xf - mu
    y = xc * lax.rsqrt(jnp.mean(xc * xc, axis=-1, keepdims=True) + EPS)
    return (y * g.astype(jnp.float32) + b.astype(jnp.float32)).astype(x.dtype)


def causal_dwconv(x, w):
    k = w.shape[0]
    return lax.conv_general_dilated(
        x, w[:, None, :].astype(x.dtype), window_strides=(1,), padding=[(k - 1, 0)],
        dimension_numbers=("NWC", "WIO", "NWC"), feature_group_count=x.shape[-1])


def conformer_conv_branch(u2, dw_w, dw_b, ln_g, ln_b, w_pw):
    a, b = jnp.split(u2, 2, axis=-1)
    h = a * jax.nn.sigmoid(b)
    h = causal_dwconv(h, dw_w) + dw_b.astype(h.dtype)
    h = layernorm(h, ln_g, ln_b)
    h = jax.nn.silu(h)
    return h @ w_pw


def _cmul_scan(e1, e2):
    a1r, a1i, b1r, b1i = e1
    a2r, a2i, b2r, b2i = e2
    ar = a2r * a1r - a2i * a1i
    ai = a2r * a1i + a2i * a1r
    br = a2r * b1r - a2i * b1i + b2r
    bi = a2r * b1i + a2i * b1r + b2i
    return ar, ai, br, bi


def s5_branch(u, a_re, a_im, log_dt, b_re, b_im, c_re, c_im, d_skip, w_glu):
    bsz, seq, _ = u.shape
    f32 = jnp.float32
    uf = u.astype(f32).reshape(bsz, seq, SSM_GROUPS, SSM_GROUP)
    ar = jnp.minimum(a_re.astype(f32), -1e-4)
    ai = a_im.astype(f32)
    dt = jnp.exp(log_dt.astype(f32))[:, None]
    mag = jnp.exp(dt * ar)
    abar_re = mag * jnp.cos(dt * ai)
    abar_im = mag * jnp.sin(dt * ai)
    den = ar * ar + ai * ai
    nr = abar_re - 1.0
    ni = abar_im
    z_re = (nr * ar + ni * ai) / den
    z_im = (ni * ar - nr * ai) / den
    br = b_re.astype(f32)
    bi = b_im.astype(f32)
    bbar_re = z_re[..., None] * br - z_im[..., None] * bi
    bbar_im = z_re[..., None] * bi + z_im[..., None] * br
    bu_re = jnp.einsum("blgh,gph->blgp", uf, bbar_re)
    bu_im = jnp.einsum("blgh,gph->blgp", uf, bbar_im)
    shape_a = (1, seq, SSM_GROUPS, SSM_STATE)
    a_t_re = jnp.broadcast_to(abar_re[None, None], shape_a)
    a_t_im = jnp.broadcast_to(abar_im[None, None], shape_a)
    _, _, xr, xi = lax.associative_scan(_cmul_scan, (a_t_re, a_t_im, bu_re, bu_im), axis=1)
    y = (jnp.einsum("blgp,ghp->blgh", xr, c_re.astype(f32))
         - jnp.einsum("blgp,ghp->blgh", xi, c_im.astype(f32)))
    y = y.reshape(bsz, seq, SSM_WIDTH) + d_skip.astype(f32) * uf.reshape(bsz, seq, SSM_WIDTH)
    y = jax.nn.gelu(y).astype(u.dtype)
    g = y @ w_glu
    ga, gb = jnp.split(g, 2, axis=-1)
    return ga * jax.nn.sigmoid(gb)


def cross_attention(h, m, w_q, w_kv, w_o):
    bsz, seq, _ = h.shape
    q = (h @ w_q).reshape(bsz, seq, XA_HEADS, XA_HEAD_DIM)
    k, v = jnp.split(m @ w_kv, 2, axis=-1)
    k = k.reshape(bsz, MEM_LEN, XA_HEADS, XA_HEAD_DIM)
    v = v.reshape(bsz, MEM_LEN, XA_HEADS, XA_HEAD_DIM)
    s = jnp.einsum("blhd,bmhd->bhlm", q, k).astype(jnp.float32) * (XA_HEAD_DIM ** -0.5)
    p = jax.nn.softmax(s, axis=-1).astype(v.dtype)
    o = jnp.einsum("bhlm,bmhd->blhd", p, v).reshape(bsz, seq, D_MODEL)
    return o @ w_o


def conv_ffn(h, w_up, dw_w, w_down):
    up = causal_dwconv(h @ w_up, dw_w)
    gate, val = jnp.split(up, 2, axis=-1)
    return (jax.nn.silu(gate) * val) @ w_down


def setup_inputs(seed: int = 0) -> dict:
    key = jax.random.key(seed)
    ks = iter(jax.random.split(key, 40))

    def nrm(shape, scale):
        return jax.random.normal(next(ks), shape, jnp.float32) * scale

    def gain(shape):
        return 1.0 + nrm(shape, 0.02)

    L = DEPTH
    n = jnp.arange(SSM_STATE, dtype=jnp.float32)
    a_re = -0.5 + nrm((L, SSM_GROUPS, SSM_STATE), 0.01)
    a_im = math.pi * n[None, None, :] + nrm((L, SSM_GROUPS, SSM_STATE), 0.01)
    log_dt = jax.random.uniform(next(ks), (L, SSM_GROUPS), jnp.float32,
                                math.log(DT_MIN), math.log(DT_MAX))
    return {
        "x": nrm((BATCH, SEQ, D_MODEL), 1.0),
        "mem": nrm((BATCH, MEM_LEN, D_MODEL), 1.0),
        "mix_norm_g": gain((L, D_MODEL)),
        "w_in": nrm((L, D_MODEL, IN_COLS), D_MODEL ** -0.5),
        "conv_dw_w": nrm((L, CONV_K, CONV_WIDTH), CONV_K ** -0.5),
        "conv_dw_b": nrm((L, CONV_WIDTH), 0.02),
        "conv_ln_g": gain((L, CONV_WIDTH)),
        "conv_ln_b": nrm((L, CONV_WIDTH), 0.02),
        "conv_w_pw": nrm((L, CONV_WIDTH, D_MODEL), CONV_WIDTH ** -0.5),
        "ssm_a_re": a_re,
        "ssm_a_im": a_im,
        "ssm_log_dt": log_dt,
        "ssm_b_re": nrm((L, SSM_GROUPS, SSM_STATE, SSM_GROUP), (2 * SSM_GROUP) ** -0.5),
        "ssm_b_im": nrm((L, SSM_GROUPS, SSM_STATE, SSM_GROUP), (2 * SSM_GROUP) ** -0.5),
        "ssm_c_re": nrm((L, SSM_GROUPS, SSM_GROUP, SSM_STATE), SSM_STATE ** -0.5),
        "ssm_c_im": nrm((L, SSM_GROUPS, SSM_GROUP, SSM_STATE), SSM_STATE ** -0.5),
        "ssm_d": nrm((L, SSM_WIDTH), 1.0),
        "ssm_w_glu": nrm((L, SSM_WIDTH, 2 * D_MODEL), SSM_WIDTH ** -0.5),
        "w_out": nrm((L, D_MODEL, D_MODEL), D_MODEL ** -0.5),
        "xa_norm_g": gain((L, D_MODEL)),
        "mem_norm_g": gain((L, D_MODEL)),
        "xa_w_q": nrm((L, D_MODEL, D_MODEL), D_MODEL ** -0.5),
        "xa_w_kv": nrm((L, D_MODEL, 2 * D_MODEL), D_MODEL ** -0.5),
        "xa_w_o": nrm((L, D_MODEL, D_MODEL), D_MODEL ** -0.5),
        "ffn_norm_g": gain((L, D_MODEL)),
        "ffn_w_up": nrm((L, D_MODEL, 2 * D_FF), D_MODEL ** -0.5),
        "ffn_dw_w": nrm((L, FFN_K, 2 * D_FF), FFN_K ** -0.5),
        "ffn_w_down": nrm((L, D_FF, D_MODEL), D_FF ** -0.5),
        "final_norm_g": gain((D_MODEL,)),
    }


def reference(x, mem, mix_norm_g, w_in, conv_dw_w, conv_dw_b, conv_ln_g, conv_ln_b, conv_w_pw,
              ssm_a_re, ssm_a_im, ssm_log_dt, ssm_b_re, ssm_b_im, ssm_c_re, ssm_c_im, ssm_d,
              ssm_w_glu, w_out, xa_norm_g, mem_norm_g, xa_w_q, xa_w_kv, xa_w_o,
              ffn_norm_g, ffn_w_up, ffn_dw_w, ffn_w_down, final_norm_g):
    split_pts = [2 * CONV_WIDTH, 2 * CONV_WIDTH + SSM_WIDTH]
    for i in range(DEPTH):
        h = rmsnorm(x, mix_norm_g[i])
        proj = h @ w_in[i]
        u_conv, u_ssm, gate_logits = jnp.split(proj, split_pts, axis=-1)
        y_a = conformer_conv_branch(u_conv, conv_dw_w[i], conv_dw_b[i], conv_ln_g[i],
                                    conv_ln_b[i], conv_w_pw[i])
        y_b = s5_branch(u_ssm, ssm_a_re[i], ssm_a_im[i], ssm_log_dt[i], ssm_b_re[i],
                        ssm_b_im[i], ssm_c_re[i], ssm_c_im[i], ssm_d[i], ssm_w_glu[i])
        g_a, g_b = jnp.split(jax.nn.sigmoid(gate_logits), 2, axis=-1)
        x = x + (g_a * y_a + g_b * y_b) @ w_out[i]
        h = rmsnorm(x, xa_norm_g[i])
        m = rmsnorm(mem, mem_norm_g[i])
        x = x + cross_attention(h, m, xa_w_q[i], xa_w_kv[i], xa_w_o[i])
        h = rmsnorm(x, ffn_norm_g[i])
        x = x + conv_ffn(h, ffn_w_up[i], ffn_dw_w[i], ffn_w_down[i])
    return rmsnorm(x, final_norm_g)
```

```python
import functools
import math

import jax
import jax.numpy as jnp
from jax import lax
from jax.experimental import pallas as pl
from jax.experimental.pallas import tpu as pltpu

D_MODEL = 2048
MEM_LEN = 256
CONV_WIDTH = 1024
CONV_K = 31
SSM_WIDTH = 1024
SSM_GROUP = 16
SSM_GROUPS = SSM_WIDTH // SSM_GROUP
SSM_STATE = 64
XA_HEADS = 4
XA_HEAD_DIM = D_MODEL // XA_HEADS
D_FF = 5632
FFN_K = 3
EPS = 1e-6

LANES = 128
SUBLANES = 8
VMEM_LIMIT = 56 * 1024 * 1024

BF16 = jnp.bfloat16
F32 = jnp.float32


def _params(*sem):
    return pltpu.CompilerParams(dimension_semantics=sem, vmem_limit_bytes=VMEM_LIMIT)


def _rms(x, g):
    return x * lax.rsqrt(jnp.mean(x * x, axis=-1, keepdims=True) + EPS) * g


def _sigmoid(x):
    return 1.0 / (1.0 + jnp.exp(-x))


def _rmsnorm_kernel(x_ref, g_ref, o_ref):
    o_ref[...] = _rms(x_ref[...], g_ref[...]).astype(o_ref.dtype)


def rmsnorm_rows(x, g, tm, out_dtype):
    m, d = x.shape
    return pl.pallas_call(
        _rmsnorm_kernel,
        grid=(m // tm,),
        in_specs=[pl.BlockSpec((tm, d), lambda i: (i, 0)),
                  pl.BlockSpec((1, d), lambda i: (0, 0))],
        out_specs=pl.BlockSpec((tm, d), lambda i: (i, 0)),
        out_shape=jax.ShapeDtypeStruct((m, d), out_dtype),
        compiler_params=_params("parallel"),
        name="rmsnorm",
    )(x, g.reshape(1, d))


def _mm_kernel(a_ref, w_ref, o_ref):
    o_ref[...] = jnp.dot(a_ref[...], w_ref[...], preferred_element_type=F32).astype(o_ref.dtype)


def matmul(a, w, tm, tn, out_dtype, name):
    m, k = a.shape
    n = w.shape[1]
    return pl.pallas_call(
        _mm_kernel,
        grid=(m // tm, n // tn),
        in_specs=[pl.BlockSpec((tm, k), lambda i, j: (i, 0)),
                  pl.BlockSpec((k, tn), lambda i, j: (0, j))],
        out_specs=pl.BlockSpec((tm, tn), lambda i, j: (i, j)),
        out_shape=jax.ShapeDtypeStruct((m, n), out_dtype),
        compiler_params=_params("parallel", "arbitrary"),
        name=name,
    )(a, w)


def _mm_res_norm_kernel(a_ref, w_ref, x_ref, g_ref, *refs, nk, emit_x):
    if emit_x:
        xo_ref, h_ref, acc_ref = refs
    else:
        h_ref, acc_ref = refs
    k = pl.program_id(1)

    @pl.when(k == 0)
    def _():
        acc_ref[...] = x_ref[...]

    acc_ref[...] += jnp.dot(a_ref[...], w_ref[...], preferred_element_type=F32)

    @pl.when(k == nk - 1)
    def _():
        xn = acc_ref[...]
        if emit_x:
            xo_ref[...] = xn
        h_ref[...] = _rms(xn, g_ref[...]).astype(h_ref.dtype)


def matmul_res_norm(a, w, x, g, tm, tk, h_dtype, emit_x, name):
    m, kdim = a.shape
    n = w.shape[1]
    nk = kdim // tk
    out_specs = [pl.BlockSpec((tm, n), lambda i, k: (i, 0))]
    out_shape = [jax.ShapeDtypeStruct((m, n), h_dtype)]
    if emit_x:
        out_specs = [pl.BlockSpec((tm, n), lambda i, k: (i, 0))] + out_specs
        out_shape = [jax.ShapeDtypeStruct((m, n), F32)] + out_shape
    return pl.pallas_call(
        functools.partial(_mm_res_norm_kernel, nk=nk, emit_x=emit_x),
        grid=(m // tm, nk),
        in_specs=[pl.BlockSpec((tm, tk), lambda i, k: (i, k)),
                  pl.BlockSpec((tk, n), lambda i, k: (k, 0)),
                  pl.BlockSpec((tm, n), lambda i, k: (i, 0)),
                  pl.BlockSpec((1, n), lambda i, k: (0, 0))],
        out_specs=out_specs,
        out_shape=out_shape,
        scratch_shapes=[pltpu.VMEM((tm, n), F32)],
        compiler_params=_params("parallel", "arbitrary"),
        name=name,
    )(a, w, x, g.reshape(1, n))


CONV_HALO = 32
CONV_RB = 32


def _conv_kernel(a_ref, b_ref, w_ref, bias_ref, lng_ref, lnb_ref, o_ref, ext_ref, y_ref, *, tm):
    nslab = CONV_WIDTH // LANES
    t = pl.program_id(1)

    @pl.when(t == 0)
    def _():
        ext_ref[:, 0:CONV_HALO, :] = jnp.zeros((nslab, CONV_HALO, LANES), F32)

    for c in range(nslab):
        sl = slice(c * LANES, (c + 1) * LANES)
        ext_ref[c, CONV_HALO:CONV_HALO + tm, :] = a_ref[:, sl] * _sigmoid(b_ref[:, sl])

    first = CONV_HALO - (CONV_K - 1)
    for c in range(nslab):
        sl = slice(c * LANES, (c + 1) * LANES)

        def row_block(rb, carry, c=c, sl=sl):
            r0 = pl.multiple_of(rb * CONV_RB, CONV_RB)
            acc = jnp.broadcast_to(bias_ref[:, sl], (CONV_RB, LANES))
            for k in range(CONV_K):
                acc = acc + w_ref[k:k + 1, sl] * ext_ref[c, pl.ds(r0 + first + k, CONV_RB), :]
            y_ref[pl.ds(r0, CONV_RB), sl] = acc
            return carry

        lax.fori_loop(0, tm // CONV_RB, row_block, 0)

    ext_ref[:, 0:CONV_HALO, :] = ext_ref[:, tm:tm + CONV_HALO, :]

    def ln_block(rb, carry):
        r0 = pl.multiple_of(rb * CONV_RB, CONV_RB)
        y = y_ref[pl.ds(r0, CONV_RB), :]
        mu = jnp.mean(y, axis=-1, keepdims=True)
        yc = y - mu
        yn = yc * lax.rsqrt(jnp.mean(yc * yc, axis=-1, keepdims=True) + EPS)
        yn = yn * lng_ref[...] + lnb_ref[...]
        o_ref[pl.ds(r0, CONV_RB), :] = (yn * _sigmoid(yn)).astype(o_ref.dtype)
        return carry

    lax.fori_loop(0, tm // CONV_RB, ln_block, 0)


def conv_branch(proj, dw_w, dw_b, ln_g, ln_b, bsz, seq, tm):
    nt = seq // tm
    cw = CONV_WIDTH
    row = lambda b, t: (b * nt + t, 0)
    vec = lambda b, t: (0, 0)
    return pl.pallas_call(
        functools.partial(_conv_kernel, tm=tm),
        grid=(bsz, nt),
        in_specs=[pl.BlockSpec((tm, cw), lambda b, t: (b * nt + t, 0)),
                  pl.BlockSpec((tm, cw), lambda b, t: (b * nt + t, 1)),
                  pl.BlockSpec((CONV_K, cw), vec),
                  pl.BlockSpec((1, cw), vec),
                  pl.BlockSpec((1, cw), vec),
                  pl.BlockSpec((1, cw), vec)],
        out_specs=pl.BlockSpec((tm, cw), row),
        out_shape=jax.ShapeDtypeStruct((bsz * seq, cw), BF16),
        scratch_shapes=[pltpu.VMEM((cw // LANES, CONV_HALO + tm, LANES), F32),
                        pltpu.VMEM((tm, cw), F32)],
        compiler_params=_params("parallel", "arbitrary"),
        name="conv_branch",
    )(proj, proj, dw_w, dw_b.reshape(1, cw), ln_g.reshape(1, cw), ln_b.reshape(1, cw))


SSM_GB = 16
SSM_UB = SSM_GB * SSM_GROUP
SSM_CB = SSM_GB * SSM_STATE
SSM_NJ = 2 * SSM_CB // LANES
SSM_PAD = 4


def _ssm_prep_kernel(ar_ref, ai_ref, ldt_ref, br_ref, bi_ref, abr_ref, abi_ref, bbr_ref, bbi_ref):
    ar = jnp.minimum(ar_ref[...], -1e-4)
    ai = ai_ref[...]
    dt = jnp.exp(ldt_ref[...])
    mag = jnp.exp(dt * ar)
    abar_re = mag * jnp.cos(dt * ai)
    abar_im = mag * jnp.sin(dt * ai)
    den = ar * ar + ai * ai
    nr = abar_re - 1.0
    ni = abar_im
    z_re = (nr * ar + ni * ai) / den
    z_im = (ni * ar - nr * ai) / den
    abr_ref[...] = abar_re
    abi_ref[...] = abar_im
    br = br_ref[...]
    bi = bi_ref[...]
    bbr_ref[...] = z_re * br - z_im * bi
    bbi_ref[...] = z_re * bi + z_im * br


def ssm_prep(a_re, a_im, log_dt, b_re, b_im):
    gp = SSM_GROUPS * SSM_STATE
    ar = a_re.reshape(1, gp)
    ai = a_im.reshape(1, gp)
    ldt = jnp.broadcast_to(log_dt[:, None], (SSM_GROUPS, SSM_STATE)).reshape(1, gp)
    brt = jnp.transpose(b_re, (2, 0, 1)).reshape(SSM_GROUP, gp)
    bit = jnp.transpose(b_im, (2, 0, 1)).reshape(SSM_GROUP, gp)
    vec = jax.ShapeDtypeStruct((1, gp), F32)
    mat = jax.ShapeDtypeStruct((SSM_GROUP, gp), F32)
    return pl.pallas_call(_ssm_prep_kernel, out_shape=[vec, vec, mat, mat], name="ssm_prep")(
        ar, ai, ldt, brt, bit)


def _ssm_kernel(u_ref, bw_ref, cw_ref, abr_ref, abi_ref, d_ref, o_ref, s_ref, st_ref, *, tt, nb):
    pitch = tt + SSM_PAD
    half = SSM_NJ // 2
    t_idx = pl.program_id(1)

    @pl.when(t_idx == 0)
    def _():
        st_ref[...] = jnp.zeros(st_ref.shape, F32)

    for b in range(nb):
        bu = jnp.dot(u_ref[b].astype(BF16), bw_ref[0], preferred_element_type=F32)
        for j in range(SSM_NJ):
            s_ref[b, j * pitch:j * pitch + tt, :] = bu[:, j * LANES:(j + 1) * LANES]

    ar = abr_ref[0]
    ai = abi_ref[0]

    def step(t, carry):
        out = []
        for b in range(nb):
            xr, xi = carry[2 * b], carry[2 * b + 1]
            sb = s_ref.at[b]
            br = sb[pl.ds(t, half, stride=pitch), :]
            bi = sb[pl.ds(half * pitch + t, half, stride=pitch), :]
            nr = ar * xr - ai * xi + br
            ni = ar * xi + ai * xr + bi
            sb[pl.ds(t, half, stride=pitch), :] = nr
            sb[pl.ds(half * pitch + t, half, stride=pitch), :] = ni
            out += [nr, ni]
        return tuple(out)

    init = tuple(st_ref[i] for i in range(2 * nb))
    fin = lax.fori_loop(0, tt, step, init, unroll=8)
    for i in range(2 * nb):
        st_ref[i] = fin[i]

    for b in range(nb):
        acc = jnp.zeros((tt, SSM_UB), F32)
        for j in range(SSM_NJ):
            xj = s_ref[b, j * pitch:j * pitch + tt, :].astype(BF16)
            part = jnp.dot(xj, cw_ref[0, j * LANES:(j + 1) * LANES, :], preferred_element_type=F32)
            acc = acc + part if j < half else acc - part
        y = acc + d_ref[...] * u_ref[b]
        o_ref[b] = jax.nn.gelu(y).astype(o_ref.dtype)


def ssm_branch(proj3, bw, cw, abr, abi, d_skip, tt):
    bsz, seq, _ = proj3.shape
    ngb = SSM_WIDTH // SSM_UB
    col0 = 2 * CONV_WIDTH // SSM_UB
    pitch = tt + SSM_PAD
    return pl.pallas_call(
        functools.partial(_ssm_kernel, tt=tt, nb=bsz),
        grid=(ngb, seq // tt),
        in_specs=[pl.BlockSpec((bsz, tt, SSM_UB), lambda g, t: (0, t, col0 + g)),
                  pl.BlockSpec((1, SSM_UB, 2 * SSM_CB), lambda g, t: (g, 0, 0)),
                  pl.BlockSpec((1, 2 * SSM_CB, SSM_UB), lambda g, t: (g, 0, 0)),
                  pl.BlockSpec((1, SUBLANES, LANES), lambda g, t: (g, 0, 0)),
                  pl.BlockSpec((1, SUBLANES, LANES), lambda g, t: (g, 0, 0)),
                  pl.BlockSpec((1, SSM_UB), lambda g, t: (0, g))],
        out_specs=pl.BlockSpec((bsz, tt, SSM_UB), lambda g, t: (0, t, g)),
        out_shape=jax.ShapeDtypeStruct((bsz, seq, SSM_WIDTH), BF16),
        scratch_shapes=[pltpu.VMEM((bsz, SSM_NJ * pitch, LANES), F32),
                        pltpu.VMEM((2 * bsz, SUBLANES, LANES), F32)],
        compiler_params=_params("parallel", "arbitrary"),
        name="ssm_branch",
    )(proj3, bw, cw, abr, abi, d_skip.reshape(1, SSM_WIDTH))


def _ssm_block_weights(bbr_t, bbi_t, c_re, c_im):
    ngb = SSM_GROUPS // SSM_GB
    eye = jnp.eye(SSM_GB, dtype=F32)

    def bblock(bt):
        v = bt.reshape(SSM_GROUP, ngb, SSM_GB, SSM_STATE)
        m = jnp.einsum("hbgp,kg->bkhgp", v, eye)
        return m.reshape(ngb, SSM_UB, SSM_CB)

    def cblock(c):
        v = c.reshape(ngb, SSM_GB, SSM_GROUP, SSM_STATE)
        m = jnp.einsum("bghp,kg->bkpgh", v, eye)
        return m.reshape(ngb, SSM_CB, SSM_UB)

    bw = jnp.concatenate([bblock(bbr_t), bblock(bbi_t)], axis=2).astype(BF16)
    cw = jnp.concatenate([cblock(c_re), cblock(c_im)], axis=1).astype(BF16)
    return bw, cw


def _zmix_kernel(hc_ref, ys_ref, wpw_ref, wga_ref, wgb_ref, la_ref, lb_ref, z_ref):
    ya = jnp.dot(hc_ref[...], wpw_ref[...], preferred_element_type=F32)
    ys = ys_ref[...]
    ga = jnp.dot(ys, wga_ref[...], preferred_element_type=F32)
    gb = jnp.dot(ys, wgb_ref[...], preferred_element_type=F32)
    yb = ga * _sigmoid(gb)
    z = _sigmoid(la_ref[...]) * ya + _sigmoid(lb_ref[...]) * yb
    z_ref[...] = z.astype(z_ref.dtype)


def zmix(hc, ys, w_pw, w_glu, proj, tm, tn):
    m = hc.shape[0]
    nj = D_MODEL // tn
    gate0 = (2 * CONV_WIDTH + SSM_WIDTH) // tn
    return pl.pallas_call(
        _zmix_kernel,
        grid=(m // tm, nj),
        in_specs=[pl.BlockSpec((tm, CONV_WIDTH), lambda i, j: (i, 0)),
                  pl.BlockSpec((tm, SSM_WIDTH), lambda i, j: (i, 0)),
                  pl.BlockSpec((CONV_WIDTH, tn), lambda i, j: (0, j)),
                  pl.BlockSpec((SSM_WIDTH, tn), lambda i, j: (0, j)),
                  pl.BlockSpec((SSM_WIDTH, tn), lambda i, j: (0, nj + j)),
                  pl.BlockSpec((tm, tn), lambda i, j: (i, gate0 + j)),
                  pl.BlockSpec((tm, tn), lambda i, j: (i, gate0 + nj + j))],
        out_specs=pl.BlockSpec((tm, tn), lambda i, j: (i, j)),
        out_shape=jax.ShapeDtypeStruct((m, D_MODEL), BF16),
        compiler_params=_params("parallel", "arbitrary"),
        name="zmix",
    )(hc, ys, w_pw, w_glu, w_glu, proj, proj)


def _attn_kernel(q_ref, k_ref, v_ref, o_ref):
    scale = XA_HEAD_DIM ** -0.5
    for h in range(XA_HEADS):
        sl = slice(h * XA_HEAD_DIM, (h + 1) * XA_HEAD_DIM)
        s = lax.dot_general(q_ref[:, sl], k_ref[:, sl], (((1,), (1,)), ((), ())),
                            preferred_element_type=F32) * scale
        s = s - jnp.max(s, axis=-1, keepdims=True)
        p = jnp.exp(s)
        p = p / jnp.sum(p, axis=-1, keepdims=True)
        o = jnp.dot(p.astype(BF16), v_ref[:, sl], preferred_element_type=F32)
        o_ref[:, sl] = o.astype(o_ref.dtype)


def attention(q, kv, bsz, seq, tm):
    nt = seq // tm
    return pl.pallas_call(
        _attn_kernel,
        grid=(bsz, nt),
        in_specs=[pl.BlockSpec((tm, D_MODEL), lambda b, t: (b * nt + t, 0)),
                  pl.BlockSpec((MEM_LEN, D_MODEL), lambda b, t: (b, 0)),
                  pl.BlockSpec((MEM_LEN, D_MODEL), lambda b, t: (b, 1))],
        out_specs=pl.BlockSpec((tm, D_MODEL), lambda b, t: (b * nt + t, 0)),
        out_shape=jax.ShapeDtypeStruct((bsz * seq, D_MODEL), BF16),
        compiler_params=_params("parallel", "arbitrary"),
        name="attention",
    )(q, kv, kv)


FFN_HALO = 8
FFN_RB = 32


def _ffn_up_kernel(h_ref, wg_ref, wv_ref, dg_ref, dv_ref, o_ref, eg_ref, ev_ref, *, tm, tn):
    nslab = tn // LANES
    t = pl.program_id(2)

    @pl.when(t == 0)
    def _():
        eg_ref[:, 0:FFN_HALO, :] = jnp.zeros((nslab, FFN_HALO, LANES), F32)
        ev_ref[:, 0:FFN_HALO, :] = jnp.zeros((nslab, FFN_HALO, LANES), F32)

    h = h_ref[...]
    ug = jnp.dot(h, wg_ref[...], preferred_element_type=F32)
    for c in range(nslab):
        eg_ref[c, FFN_HALO:FFN_HALO + tm, :] = ug[:, c * LANES:(c + 1) * LANES]
    uv = jnp.dot(h, wv_ref[...], preferred_element_type=F32)
    for c in range(nslab):
        ev_ref[c, FFN_HALO:FFN_HALO + tm, :] = uv[:, c * LANES:(c + 1) * LANES]

    first = FFN_HALO - (FFN_K - 1)
    for c in range(nslab):
        sl = slice(c * LANES, (c + 1) * LANES)

        def row_block(rb, carry, c=c, sl=sl):
            r0 = pl.multiple_of(rb * FFN_RB, FFN_RB)
            g = dg_ref[0:1, sl] * eg_ref[c, pl.ds(r0 + first, FFN_RB), :]
            v = dv_ref[0:1, sl] * ev_ref[c, pl.ds(r0 + first, FFN_RB), :]
            for k in range(1, FFN_K):
                g = g + dg_ref[k:k + 1, sl] * eg_ref[c, pl.ds(r0 + first + k, FFN_RB), :]
                v = v + dv_ref[k:k + 1, sl] * ev_ref[c, pl.ds(r0 + first + k, FFN_RB), :]
            o_ref[pl.ds(r0, FFN_RB), sl] = (g * _sigmoid(g) * v).astype(o_ref.dtype)
            return carry

        lax.fori_loop(0, tm // FFN_RB, row_block, 0)

    eg_ref[:, 0:FFN_HALO, :] = eg_ref[:, tm:tm + FFN_HALO, :]
    ev_ref[:, 0:FFN_HALO, :] = ev_ref[:, tm:tm + FFN_HALO, :]


def ffn_up(h, w_up, dw_w, bsz, seq, tm, tn):
    nt = seq // tm
    nj = D_FF // tn
    return pl.pallas_call(
        functools.partial(_ffn_up_kernel, tm=tm, tn=tn),
        grid=(nj, bsz, nt),
        in_specs=[pl.BlockSpec((tm, D_MODEL), lambda j, b, t: (b * nt + t, 0)),
                  pl.BlockSpec((D_MODEL, tn), lambda j, b, t: (0, j)),
                  pl.BlockSpec((D_MODEL, tn), lambda j, b, t: (0, nj + j)),
                  pl.BlockSpec((FFN_K, tn), lambda j, b, t: (0, j)),
                  pl.BlockSpec((FFN_K, tn), lambda j, b, t: (0, nj + j))],
        out_specs=pl.BlockSpec((tm, tn), lambda j, b, t: (b * nt + t, j)),
        out_shape=jax.ShapeDtypeStruct((bsz * seq, D_FF), BF16),
        scratch_shapes=[pltpu.VMEM((tn // LANES, FFN_HALO + tm, LANES), F32),
                        pltpu.VMEM((tn // LANES, FFN_HALO + tm, LANES), F32)],
        compiler_params=_params("parallel", "parallel", "arbitrary"),
        name="ffn_up",
    )(h, w_up, w_up, dw_w, dw_w)


def kernel(x, mem, mix_norm_g, w_in, conv_dw_w, conv_dw_b, conv_ln_g, conv_ln_b, conv_w_pw,
           ssm_a_re, ssm_a_im, ssm_log_dt, ssm_b_re, ssm_b_im, ssm_c_re, ssm_c_im, ssm_d,
           ssm_w_glu, w_out, xa_norm_g, mem_norm_g, xa_w_q, xa_w_kv, xa_w_o,
           ffn_norm_g, ffn_w_up, ffn_dw_w, ffn_w_down, final_norm_g):
    bsz, seq, d = x.shape
    depth = w_in.shape[0]
    rows = bsz * seq
    xf = x.reshape(rows, d)
    memf = mem.reshape(bsz * MEM_LEN, d)
    ngb = SSM_GROUPS // SSM_GB

    h = rmsnorm_rows(xf, mix_norm_g[0], 512, BF16)
    out = None
    for i in range(depth):
        proj = matmul(h, w_in[i].astype(BF16), 1024, 512, F32, "in_proj")
        hc = conv_branch(proj, conv_dw_w[i], conv_dw_b[i], conv_ln_g[i], conv_ln_b[i], bsz, seq, 512)
        abr, abi, bbr_t, bbi_t = ssm_prep(ssm_a_re[i], ssm_a_im[i], ssm_log_dt[i], ssm_b_re[i], ssm_b_im[i])
        bw, cw = _ssm_block_weights(bbr_t, bbi_t, ssm_c_re[i], ssm_c_im[i])
        ys = ssm_branch(proj.reshape(bsz, seq, -1), bw, cw,
                        abr.reshape(ngb, SUBLANES, LANES), abi.reshape(ngb, SUBLANES, LANES),
                        ssm_d[i], 512).reshape(rows, SSM_WIDTH)
        z = zmix(hc, ys, conv_w_pw[i].astype(BF16), ssm_w_glu[i].astype(BF16), proj, 512, 512)
        xf, h = matmul_res_norm(z, w_out[i].astype(BF16), xf, xa_norm_g[i], 512, D_MODEL, BF16, True, "out_proj")
        q = matmul(h, xa_w_q[i].astype(BF16), 1024, 512, BF16, "q_proj")
        mn = rmsnorm_rows(memf, mem_norm_g[i], 256, BF16)
        kv = matmul(mn, xa_w_kv[i].astype(BF16), 512, 512, BF16, "kv_proj")
        o = attention(q, kv, bsz, seq, 512)
        xf, h = matmul_res_norm(o, xa_w_o[i].astype(BF16), xf, ffn_norm_g[i], 512, D_MODEL, BF16, True, "o_proj")
        act = ffn_up(h, ffn_w_up[i].astype(BF16), ffn_dw_w[i], bsz, seq, 512, 512)
        if i + 1 < depth:
            xf, h = matmul_res_norm(act, ffn_w_down[i].astype(BF16), xf, mix_norm_g[i + 1],
                                    512, D_FF // 4, BF16, True, "down_proj")
        else:
            out = matmul_res_norm(act, ffn_w_down[i].astype(BF16), xf, final_norm_g,
                                  512, D_FF // 4, F32, False, "down_proj_final")[0]
    return out.reshape(bsz, seq, d)
```

```python
import functools
import math

import jax
import jax.numpy as jnp
from jax import lax
from jax.experimental import pallas as pl
from jax.experimental.pallas import tpu as pltpu

D_MODEL = 2048
MEM_LEN = 256
CONV_WIDTH = 1024
CONV_K = 31
SSM_WIDTH = 1024
SSM_GROUP = 16
SSM_GROUPS = SSM_WIDTH // SSM_GROUP
SSM_STATE = 64
XA_HEADS = 4
XA_HEAD_DIM = D_MODEL // XA_HEADS
D_FF = 5632
FFN_K = 3
EPS = 1e-6

LANES = 128
SUBLANES = 8
VMEM_LIMIT = 56 * 1024 * 1024

BF16 = jnp.bfloat16
F32 = jnp.float32


def _params(*sem):
    return pltpu.CompilerParams(dimension_semantics=sem, vmem_limit_bytes=VMEM_LIMIT)


def _rms(x, g):
    return x * lax.rsqrt(jnp.mean(x * x, axis=-1, keepdims=True) + EPS) * g


def _sigmoid(x):
    return 1.0 / (1.0 + jnp.exp(-x))


def _rmsnorm_kernel(x_ref, g_ref, o_ref):
    o_ref[...] = _rms(x_ref[...], g_ref[...]).astype(o_ref.dtype)


def rmsnorm_rows(x, g, tm, out_dtype):
    m, d = x.shape
    return pl.pallas_call(
        _rmsnorm_kernel,
        grid=(m // tm,),
        in_specs=[pl.BlockSpec((tm, d), lambda i: (i, 0)),
                  pl.BlockSpec((1, d), lambda i: (0, 0))],
        out_specs=pl.BlockSpec((tm, d), lambda i: (i, 0)),
        out_shape=jax.ShapeDtypeStruct((m, d), out_dtype),
        compiler_params=_params("parallel"),
        name="rmsnorm",
    )(x, g.reshape(1, d))


def _mm_kernel(a_ref, w_ref, o_ref):
    o_ref[...] = jnp.dot(a_ref[...], w_ref[...], preferred_element_type=F32).astype(o_ref.dtype)


def matmul(a, w, li, tm, tn, out_dtype, name):
    m, k = a.shape
    n = w.shape[2]
    return pl.pallas_call(
        _mm_kernel,
        grid=(m // tm, n // tn),
        in_specs=[pl.BlockSpec((tm, k), lambda i, j: (i, 0)),
                  pl.BlockSpec((None, k, tn), lambda i, j: (li, 0, j))],
        out_specs=pl.BlockSpec((tm, tn), lambda i, j: (i, j)),
        out_shape=jax.ShapeDtypeStruct((m, n), out_dtype),
        compiler_params=_params("parallel", "arbitrary"),
        name=name,
    )(a, w)


def _mm_wcast_kernel(a_ref, w_ref, o_ref, wb_ref):
    @pl.when(pl.program_id(1) == 0)
    def _():
        wb_ref[...] = w_ref[...].astype(BF16)

    o_ref[...] = jnp.dot(a_ref[...], wb_ref[...], preferred_element_type=F32).astype(o_ref.dtype)


def matmul_wcast(a, w, li, tm, tn, out_dtype, name):
    m, k = a.shape
    n = w.shape[2]
    return pl.pallas_call(
        _mm_wcast_kernel,
        grid=(n // tn, m // tm),
        in_specs=[pl.BlockSpec((tm, k), lambda j, i: (i, 0)),
                  pl.BlockSpec((None, k, tn), lambda j, i: (li, 0, j))],
        out_specs=pl.BlockSpec((tm, tn), lambda j, i: (i, j)),
        out_shape=jax.ShapeDtypeStruct((m, n), out_dtype),
        scratch_shapes=[pltpu.VMEM((k, tn), BF16)],
        compiler_params=_params("parallel", "arbitrary"),
        name=name,
    )(a, w)


def _mm_res_norm_kernel(a_ref, w_ref, x_ref, g_ref, *refs, nk, emit_x):
    if emit_x:
        xo_ref, h_ref, acc_ref = refs
    else:
        h_ref, acc_ref = refs
    k = pl.program_id(1)

    @pl.when(k == 0)
    def _():
        acc_ref[...] = x_ref[...]

    acc_ref[...] += jnp.dot(a_ref[...], w_ref[...], preferred_element_type=F32)

    @pl.when(k == nk - 1)
    def _():
        xn = acc_ref[...]
        if emit_x:
            xo_ref[...] = xn
        h_ref[...] = _rms(xn, g_ref[...]).astype(h_ref.dtype)


def matmul_res_norm(a, w, li, x, g, tm, tk, h_dtype, emit_x, name):
    m, kdim = a.shape
    n = w.shape[2]
    nk = kdim // tk
    out_specs = [pl.BlockSpec((tm, n), lambda i, k: (i, 0))]
    out_shape = [jax.ShapeDtypeStruct((m, n), h_dtype)]
    if emit_x:
        out_specs = [pl.BlockSpec((tm, n), lambda i, k: (i, 0))] + out_specs
        out_shape = [jax.ShapeDtypeStruct((m, n), F32)] + out_shape
    return pl.pallas_call(
        functools.partial(_mm_res_norm_kernel, nk=nk, emit_x=emit_x),
        grid=(m // tm, nk),
        in_specs=[pl.BlockSpec((tm, tk), lambda i, k: (i, k)),
                  pl.BlockSpec((None, tk, n), lambda i, k: (li, k, 0)),
                  pl.BlockSpec((tm, n), lambda i, k: (i, 0)),
                  pl.BlockSpec((1, n), lambda i, k: (0, 0))],
        out_specs=out_specs,
        out_shape=out_shape,
        scratch_shapes=[pltpu.VMEM((tm, n), F32)],
        compiler_params=_params("parallel", "arbitrary"),
        name=name,
    )(a, w, x, g.reshape(1, n))


CONV_HALO = 32
CONV_RB = 32


def _conv_kernel(a_ref, b_ref, w_ref, bias_ref, lng_ref, lnb_ref, o_ref, ext_ref, y_ref, *, tm):
    nslab = CONV_WIDTH // LANES
    t = pl.program_id(1)

    @pl.when(t == 0)
    def _():
        ext_ref[:, 0:CONV_HALO, :] = jnp.zeros((nslab, CONV_HALO, LANES), F32)

    for c in range(nslab):
        sl = slice(c * LANES, (c + 1) * LANES)
        ext_ref[c, CONV_HALO:CONV_HALO + tm, :] = a_ref[:, sl] * _sigmoid(b_ref[:, sl])

    first = CONV_HALO - (CONV_K - 1)
    for c in range(nslab):
        sl = slice(c * LANES, (c + 1) * LANES)

        def row_block(rb, carry, c=c, sl=sl):
            r0 = pl.multiple_of(rb * CONV_RB, CONV_RB)
            acc = [jnp.broadcast_to(bias_ref[:, sl], (CONV_RB, LANES)), None]
            for k in range(CONV_K):
                term = w_ref[k:k + 1, sl] * ext_ref[c, pl.ds(r0 + first + k, CONV_RB), :]
                acc[k % 2] = term if acc[k % 2] is None else acc[k % 2] + term
            y_ref[pl.ds(r0, CONV_RB), sl] = acc[0] + acc[1]
            return carry

        lax.fori_loop(0, tm // CONV_RB, row_block, 0, unroll=2)

    ext_ref[:, 0:CONV_HALO, :] = ext_ref[:, tm:tm + CONV_HALO, :]

    def ln_block(rb, carry):
        r0 = pl.multiple_of(rb * CONV_RB, CONV_RB)
        y = y_ref[pl.ds(r0, CONV_RB), :]
        mu = jnp.mean(y, axis=-1, keepdims=True)
        yc = y - mu
        yn = yc * lax.rsqrt(jnp.mean(yc * yc, axis=-1, keepdims=True) + EPS)
        yn = yn * lng_ref[...] + lnb_ref[...]
        o_ref[pl.ds(r0, CONV_RB), :] = (yn * _sigmoid(yn)).astype(o_ref.dtype)
        return carry

    lax.fori_loop(0, tm // CONV_RB, ln_block, 0, unroll=4)


def conv_branch(proj, dw_w, dw_b, ln_g, ln_b, bsz, seq, tm):
    nt = seq // tm
    cw = CONV_WIDTH
    row = lambda b, t: (b * nt + t, 0)
    vec = lambda b, t: (0, 0)
    return pl.pallas_call(
        functools.partial(_conv_kernel, tm=tm),
        grid=(bsz, nt),
        in_specs=[pl.BlockSpec((tm, cw), lambda b, t: (b * nt + t, 0)),
                  pl.BlockSpec((tm, cw), lambda b, t: (b * nt + t, 1)),
                  pl.BlockSpec((CONV_K, cw), vec),
                  pl.BlockSpec((1, cw), vec),
                  pl.BlockSpec((1, cw), vec),
                  pl.BlockSpec((1, cw), vec)],
        out_specs=pl.BlockSpec((tm, cw), row),
        out_shape=jax.ShapeDtypeStruct((bsz * seq, cw), BF16),
        scratch_shapes=[pltpu.VMEM((cw // LANES, CONV_HALO + tm, LANES), F32),
                        pltpu.VMEM((tm, cw), F32)],
        compiler_params=_params("parallel", "arbitrary"),
        name="conv_branch",
    )(proj, proj, dw_w, dw_b.reshape(1, cw), ln_g.reshape(1, cw), ln_b.reshape(1, cw))


SSM_GB = 16
SSM_UB = SSM_GB * SSM_GROUP
SSM_CB = SSM_GB * SSM_STATE
SSM_NJ = 2 * SSM_CB // LANES
SSM_PAD = 4


def _ssm_prep_kernel(ar_ref, ai_ref, ldt_ref, br_ref, bi_ref, abr_ref, abi_ref, bbr_ref, bbi_ref):
    ar = jnp.minimum(ar_ref[...], -1e-4)
    ai = ai_ref[...]
    dt = jnp.exp(ldt_ref[...])
    mag = jnp.exp(dt * ar)
    abar_re = mag * jnp.cos(dt * ai)
    abar_im = mag * jnp.sin(dt * ai)
    den = ar * ar + ai * ai
    nr = abar_re - 1.0
    ni = abar_im
    z_re = (nr * ar + ni * ai) / den
    z_im = (ni * ar - nr * ai) / den
    abr_ref[...] = abar_re
    abi_ref[...] = abar_im
    br = br_ref[...]
    bi = bi_ref[...]
    bbr_ref[...] = z_re * br - z_im * bi
    bbi_ref[...] = z_re * bi + z_im * br


def ssm_prep(a_re, a_im, log_dt, b_re, b_im):
    gp = SSM_GROUPS * SSM_STATE
    ar = a_re.reshape(1, gp)
    ai = a_im.reshape(1, gp)
    ldt = jnp.broadcast_to(log_dt[:, None], (SSM_GROUPS, SSM_STATE)).reshape(1, gp)
    brt = jnp.transpose(b_re, (2, 0, 1)).reshape(SSM_GROUP, gp)
    bit = jnp.transpose(b_im, (2, 0, 1)).reshape(SSM_GROUP, gp)
    vec = jax.ShapeDtypeStruct((1, gp), F32)
    mat = jax.ShapeDtypeStruct((SSM_GROUP, gp), F32)
    return pl.pallas_call(_ssm_prep_kernel, out_shape=[vec, vec, mat, mat], name="ssm_prep")(
        ar, ai, ldt, brt, bit)


def _ssm_kernel(u_ref, bw_ref, cw_ref, abr_ref, abi_ref, d_ref, o_ref, s_ref, st_ref, *, tt, nb):
    pitch = tt + SSM_PAD
    half = SSM_NJ // 2
    t_idx = pl.program_id(1)

    @pl.when(t_idx == 0)
    def _():
        st_ref[...] = jnp.zeros(st_ref.shape, F32)

    for b in range(nb):
        bu = jnp.dot(u_ref[b].astype(BF16), bw_ref[0], preferred_element_type=F32)
        for j in range(SSM_NJ):
            s_ref[b, j * pitch:j * pitch + tt, :] = bu[:, j * LANES:(j + 1) * LANES]

    ar = abr_ref[0]
    ai = abi_ref[0]

    def step(t, carry):
        out = []
        for b in range(nb):
            xr, xi = carry[2 * b], carry[2 * b + 1]
            sb = s_ref.at[b]
            br = sb[pl.ds(t, half, stride=pitch), :]
            bi = sb[pl.ds(half * pitch + t, half, stride=pitch), :]
            nr = ar * xr - ai * xi + br
            ni = ar * xi + ai * xr + bi
            sb[pl.ds(t, half, stride=pitch), :] = nr
            sb[pl.ds(half * pitch + t, half, stride=pitch), :] = ni
            out += [nr, ni]
        return tuple(out)

    init = tuple(st_ref[i] for i in range(2 * nb))
    fin = lax.fori_loop(0, tt, step, init, unroll=8)
    for i in range(2 * nb):
        st_ref[i] = fin[i]

    for b in range(nb):
        acc = jnp.zeros((tt, SSM_UB), F32)
        for j in range(0, SSM_NJ, 2):
            xj = jnp.concatenate([s_ref[b, j * pitch:j * pitch + tt, :].astype(BF16),
                                  s_ref[b, (j + 1) * pitch:(j + 1) * pitch + tt, :].astype(BF16)], axis=1)
            part = jnp.dot(xj, cw_ref[0, j * LANES:(j + 2) * LANES, :], preferred_element_type=F32)
            acc = acc + part if j < half else acc - part
        y = acc + d_ref[...] * u_ref[b]
        o_ref[b] = jax.nn.gelu(y).astype(o_ref.dtype)


def ssm_branch(proj3, bw, cw, abr, abi, d_skip, tt):
    bsz, seq, _ = proj3.shape
    ngb = SSM_WIDTH // SSM_UB
    col0 = 2 * CONV_WIDTH // SSM_UB
    pitch = tt + SSM_PAD
    return pl.pallas_call(
        functools.partial(_ssm_kernel, tt=tt, nb=bsz),
        grid=(ngb, seq // tt),
        in_specs=[pl.BlockSpec((bsz, tt, SSM_UB), lambda g, t: (0, t, col0 + g)),
                  pl.BlockSpec((1, SSM_UB, 2 * SSM_CB), lambda g, t: (g, 0, 0)),
                  pl.BlockSpec((1, 2 * SSM_CB, SSM_UB), lambda g, t: (g, 0, 0)),
                  pl.BlockSpec((1, SUBLANES, LANES), lambda g, t: (g, 0, 0)),
                  pl.BlockSpec((1, SUBLANES, LANES), lambda g, t: (g, 0, 0)),
                  pl.BlockSpec((1, SSM_UB), lambda g, t: (0, g))],
        out_specs=pl.BlockSpec((bsz, tt, SSM_UB), lambda g, t: (0, t, g)),
        out_shape=jax.ShapeDtypeStruct((bsz, seq, SSM_WIDTH), BF16),
        scratch_shapes=[pltpu.VMEM((bsz, SSM_NJ * pitch, LANES), F32),
                        pltpu.VMEM((2 * bsz, SUBLANES, LANES), F32)],
        compiler_params=_params("parallel", "arbitrary"),
        name="ssm_branch",
    )(proj3, bw, cw, abr, abi, d_skip.reshape(1, SSM_WIDTH))


def _ssm_block_weights(bbr_t, bbi_t, c_re, c_im):
    ngb = SSM_GROUPS // SSM_GB
    eye = jnp.eye(SSM_GB, dtype=F32)

    def bblock(bt):
        v = bt.reshape(SSM_GROUP, ngb, SSM_GB, SSM_STATE)
        m = jnp.einsum("hbgp,kg->bkhgp", v, eye)
        return m.reshape(ngb, SSM_UB, SSM_CB)

    def cblock(c):
        v = c.reshape(ngb, SSM_GB, SSM_GROUP, SSM_STATE)
        m = jnp.einsum("bghp,kg->bkpgh", v, eye)
        return m.reshape(ngb, SSM_CB, SSM_UB)

    bw = jnp.concatenate([bblock(bbr_t), bblock(bbi_t)], axis=2).astype(BF16)
    cw = jnp.concatenate([cblock(c_re), cblock(c_im)], axis=1).astype(BF16)
    return bw, cw


def _zmix_kernel(hc_ref, ys_ref, wpw_ref, wga_ref, wgb_ref, la_ref, lb_ref, z_ref):
    ya = jnp.dot(hc_ref[...], wpw_ref[...], preferred_element_type=F32)
    ys = ys_ref[...]
    ga = jnp.dot(ys, wga_ref[...], preferred_element_type=F32)
    gb = jnp.dot(ys, wgb_ref[...], preferred_element_type=F32)
    yb = ga * _sigmoid(gb)
    z = _sigmoid(la_ref[...]) * ya + _sigmoid(lb_ref[...]) * yb
    z_ref[...] = z.astype(z_ref.dtype)


def zmix(hc, ys, w_pw, w_glu, li, proj, tm, tn):
    m = hc.shape[0]
    nj = D_MODEL // tn
    gate0 = (2 * CONV_WIDTH + SSM_WIDTH) // tn
    return pl.pallas_call(
        _zmix_kernel,
        grid=(m // tm, nj),
        in_specs=[pl.BlockSpec((tm, CONV_WIDTH), lambda i, j: (i, 0)),
                  pl.BlockSpec((tm, SSM_WIDTH), lambda i, j: (i, 0)),
                  pl.BlockSpec((None, CONV_WIDTH, tn), lambda i, j: (li, 0, j)),
                  pl.BlockSpec((None, SSM_WIDTH, tn), lambda i, j: (li, 0, j)),
                  pl.BlockSpec((None, SSM_WIDTH, tn), lambda i, j: (li, 0, nj + j)),
                  pl.BlockSpec((tm, tn), lambda i, j: (i, gate0 + j)),
                  pl.BlockSpec((tm, tn), lambda i, j: (i, gate0 + nj + j))],
        out_specs=pl.BlockSpec((tm, tn), lambda i, j: (i, j)),
        out_shape=jax.ShapeDtypeStruct((m, D_MODEL), BF16),
        compiler_params=_params("parallel", "arbitrary"),
        name="zmix",
    )(hc, ys, w_pw, w_glu, w_glu, proj, proj)


def _attn_kernel(q_ref, k_ref, v_ref, o_ref):
    scale = XA_HEAD_DIM ** -0.5
    for h in range(XA_HEADS):
        sl = slice(h * XA_HEAD_DIM, (h + 1) * XA_HEAD_DIM)
        s = lax.dot_general(q_ref[:, sl], k_ref[:, sl], (((1,), (1,)), ((), ())),
                            preferred_element_type=F32) * scale
        s = s - jnp.max(s, axis=-1, keepdims=True)
        p = jnp.exp(s)
        p = p / jnp.sum(p, axis=-1, keepdims=True)
        o = jnp.dot(p.astype(BF16), v_ref[:, sl], preferred_element_type=F32)
        o_ref[:, sl] = o.astype(o_ref.dtype)


def attention(q, kv, bsz, seq, tm):
    nt = seq // tm
    return pl.pallas_call(
        _attn_kernel,
        grid=(bsz, nt),
        in_specs=[pl.BlockSpec((tm, D_MODEL), lambda b, t: (b * nt + t, 0)),
                  pl.BlockSpec((MEM_LEN, D_MODEL), lambda b, t: (b, 0)),
                  pl.BlockSpec((MEM_LEN, D_MODEL), lambda b, t: (b, 1))],
        out_specs=pl.BlockSpec((tm, D_MODEL), lambda b, t: (b * nt + t, 0)),
        out_shape=jax.ShapeDtypeStruct((bsz * seq, D_MODEL), BF16),
        compiler_params=_params("parallel", "arbitrary"),
        name="attention",
    )(q, kv, kv)


FFN_HALO = 8
FFN_RB = 32


MXU_COLS = 256


def _ffn_up_kernel(h_ref, wg_ref, wv_ref, dg_ref, dv_ref, o_ref, wgb_ref, wvb_ref, eg_ref, ev_ref,
                   *, tm, tn, nt):
    nslab = tn // LANES
    s = pl.program_id(1)

    @pl.when(s == 0)
    def _():
        wgb_ref[...] = wg_ref[...].astype(BF16)
        wvb_ref[...] = wv_ref[...].astype(BF16)

    @pl.when(s % nt == 0)
    def _():
        eg_ref[:, 0:FFN_HALO, :] = jnp.zeros((nslab, FFN_HALO, LANES), F32)
        ev_ref[:, 0:FFN_HALO, :] = jnp.zeros((nslab, FFN_HALO, LANES), F32)

    h = h_ref[...]
    for e_ref, wb_ref in ((eg_ref, wgb_ref), (ev_ref, wvb_ref)):
        for c2 in range(tn // MXU_COLS):
            u = jnp.dot(h, wb_ref[:, c2 * MXU_COLS:(c2 + 1) * MXU_COLS], preferred_element_type=F32)
            for half in range(MXU_COLS // LANES):
                c = c2 * (MXU_COLS // LANES) + half
                e_ref[c, FFN_HALO:FFN_HALO + tm, :] = u[:, half * LANES:(half + 1) * LANES]

    first = FFN_HALO - (FFN_K - 1)
    for c in range(nslab):
        sl = slice(c * LANES, (c + 1) * LANES)
        for rb in range(tm // FFN_RB):
            r0 = rb * FFN_RB
            g = dg_ref[0:1, sl] * eg_ref[c, first + r0:first + r0 + FFN_RB, :]
            v = dv_ref[0:1, sl] * ev_ref[c, first + r0:first + r0 + FFN_RB, :]
            for k in range(1, FFN_K):
                g = g + dg_ref[k:k + 1, sl] * eg_ref[c, first + r0 + k:first + r0 + k + FFN_RB, :]
                v = v + dv_ref[k:k + 1, sl] * ev_ref[c, first + r0 + k:first + r0 + k + FFN_RB, :]
            o_ref[r0:r0 + FFN_RB, sl] = (g * _sigmoid(g) * v).astype(o_ref.dtype)

    eg_ref[:, 0:FFN_HALO, :] = eg_ref[:, tm:tm + FFN_HALO, :]
    ev_ref[:, 0:FFN_HALO, :] = ev_ref[:, tm:tm + FFN_HALO, :]


def ffn_up(h, w_up, dw_w, li, bsz, seq, tm, tn):
    nt = seq // tm
    nj = D_FF // tn
    slabs = (tn // LANES, FFN_HALO + tm, LANES)
    return pl.pallas_call(
        functools.partial(_ffn_up_kernel, tm=tm, tn=tn, nt=nt),
        grid=(nj, bsz * nt),
        in_specs=[pl.BlockSpec((tm, D_MODEL), lambda j, s: (s, 0)),
                  pl.BlockSpec((None, D_MODEL, tn), lambda j, s: (li, 0, j)),
                  pl.BlockSpec((None, D_MODEL, tn), lambda j, s: (li, 0, nj + j)),
                  pl.BlockSpec((None, FFN_K, tn), lambda j, s: (li, 0, j)),
                  pl.BlockSpec((None, FFN_K, tn), lambda j, s: (li, 0, nj + j))],
        out_specs=pl.BlockSpec((tm, tn), lambda j, s: (s, j)),
        out_shape=jax.ShapeDtypeStruct((bsz * seq, D_FF), BF16),
        scratch_shapes=[pltpu.VMEM((D_MODEL, tn), BF16),
                        pltpu.VMEM((D_MODEL, tn), BF16),
                        pltpu.VMEM(slabs, F32),
                        pltpu.VMEM(slabs, F32)],
        compiler_params=_params("parallel", "arbitrary"),
        name="ffn_up",
    )(h, w_up, w_up, dw_w, dw_w)


def kernel(x, mem, mix_norm_g, w_in, conv_dw_w, conv_dw_b, conv_ln_g, conv_ln_b, conv_w_pw,
           ssm_a_re, ssm_a_im, ssm_log_dt, ssm_b_re, ssm_b_im, ssm_c_re, ssm_c_im, ssm_d,
           ssm_w_glu, w_out, xa_norm_g, mem_norm_g, xa_w_q, xa_w_kv, xa_w_o,
           ffn_norm_g, ffn_w_up, ffn_dw_w, ffn_w_down, final_norm_g):
    bsz, seq, d = x.shape
    depth = w_in.shape[0]
    rows = bsz * seq
    xf = x.reshape(rows, d)
    memf = mem.reshape(bsz * MEM_LEN, d)
    ngb = SSM_GROUPS // SSM_GB

    w_pw_b = conv_w_pw.astype(BF16)
    w_glu_b = ssm_w_glu.astype(BF16)
    w_out_b = w_out.astype(BF16)
    w_q_b = xa_w_q.astype(BF16)
    w_kv_b = xa_w_kv.astype(BF16)
    w_o_b = xa_w_o.astype(BF16)
    w_down_b = ffn_w_down.astype(BF16)

    h = rmsnorm_rows(xf, mix_norm_g[0], 512, BF16)
    out = None
    for i in range(depth):
        proj = matmul_wcast(h, w_in, i, 1024, 1024, F32, "in_proj")
        hc = conv_branch(proj, conv_dw_w[i], conv_dw_b[i], conv_ln_g[i], conv_ln_b[i], bsz, seq, 512)
        abr, abi, bbr_t, bbi_t = ssm_prep(ssm_a_re[i], ssm_a_im[i], ssm_log_dt[i], ssm_b_re[i], ssm_b_im[i])
        bw, cw = _ssm_block_weights(bbr_t, bbi_t, ssm_c_re[i], ssm_c_im[i])
        ys = ssm_branch(proj.reshape(bsz, seq, -1), bw, cw,
                        abr.reshape(ngb, SUBLANES, LANES), abi.reshape(ngb, SUBLANES, LANES),
                        ssm_d[i], 512).reshape(rows, SSM_WIDTH)
        z = zmix(hc, ys, w_pw_b, w_glu_b, i, proj, 512, 512)
        xf, h = matmul_res_norm(z, w_out_b, i, xf, xa_norm_g[i], 512, D_MODEL, BF16, True, "out_proj")
        q = matmul(h, w_q_b, i, 1024, 512, BF16, "q_proj")
        mn = rmsnorm_rows(memf, mem_norm_g[i], 256, BF16)
        kv = matmul(mn, w_kv_b, i, 512, 512, BF16, "kv_proj")
        o = attention(q, kv, bsz, seq, 512)
        xf, h = matmul_res_norm(o, w_o_b, i, xf, ffn_norm_g[i], 512, D_MODEL, BF16, True, "o_proj")
        act = ffn_up(h, ffn_w_up, ffn_dw_w, i, bsz, seq, 512, 512)
        if i + 1 < depth:
            xf, h = matmul_res_norm(act, w_down_b, i, xf, mix_norm_g[i + 1],
                                    512, D_FF // 4, BF16, True, "down_proj")
        else:
            out = matmul_res_norm(act, w_down_b, i, xf, final_norm_g,
                                  512, D_FF // 4, F32, False, "down_proj_final")[0]
    return out.reshape(bsz, seq, d)
```

```python
import functools
import math

import jax
import jax.numpy as jnp
from jax import lax
from jax.experimental import pallas as pl
from jax.experimental.pallas import tpu as pltpu

D_MODEL = 2048
MEM_LEN = 256
CONV_WIDTH = 1024
CONV_K = 31
SSM_WIDTH = 1024
SSM_GROUP = 16
SSM_GROUPS = SSM_WIDTH // SSM_GROUP
SSM_STATE = 64
XA_HEADS = 4
XA_HEAD_DIM = D_MODEL // XA_HEADS
D_FF = 5632
FFN_K = 3
EPS = 1e-6

LANES = 128
SUBLANES = 8
VMEM_LIMIT = 56 * 1024 * 1024

BF16 = jnp.bfloat16
F32 = jnp.float32


def _params(*sem):
    return pltpu.CompilerParams(dimension_semantics=sem, vmem_limit_bytes=VMEM_LIMIT)


def _rms(x, g):
    return x * lax.rsqrt(jnp.mean(x * x, axis=-1, keepdims=True) + EPS) * g


def _sigmoid(x):
    return 0.5 * jnp.tanh(0.5 * x) + 0.5


def _silu(x):
    hx = 0.5 * x
    return hx * jnp.tanh(hx) + hx


def _rmsnorm_kernel(x_ref, g_ref, o_ref):
    o_ref[...] = _rms(x_ref[...], g_ref[...]).astype(o_ref.dtype)


def rmsnorm_rows(x, g, tm, out_dtype):
    m, d = x.shape
    return pl.pallas_call(
        _rmsnorm_kernel,
        grid=(m // tm,),
        in_specs=[pl.BlockSpec((tm, d), lambda i: (i, 0)),
                  pl.BlockSpec((1, d), lambda i: (0, 0))],
        out_specs=pl.BlockSpec((tm, d), lambda i: (i, 0)),
        out_shape=jax.ShapeDtypeStruct((m, d), out_dtype),
        compiler_params=_params("parallel"),
        name="rmsnorm",
    )(x, g.reshape(1, d))


def _mm_kernel(a_ref, w_ref, o_ref):
    o_ref[...] = jnp.dot(a_ref[...], w_ref[...], preferred_element_type=F32).astype(o_ref.dtype)


def matmul(a, w, li, tm, tn, out_dtype, name):
    m, k = a.shape
    n = w.shape[2]
    return pl.pallas_call(
        _mm_kernel,
        grid=(m // tm, n // tn),
        in_specs=[pl.BlockSpec((tm, k), lambda i, j: (i, 0)),
                  pl.BlockSpec((None, k, tn), lambda i, j: (li, 0, j))],
        out_specs=pl.BlockSpec((tm, tn), lambda i, j: (i, j)),
        out_shape=jax.ShapeDtypeStruct((m, n), out_dtype),
        compiler_params=_params("parallel", "arbitrary"),
        name=name,
    )(a, w)


def _mm_wcast_kernel(a_ref, w_ref, o_ref, wb_ref):
    @pl.when(pl.program_id(1) == 0)
    def _():
        wb_ref[...] = w_ref[...].astype(BF16)

    o_ref[...] = jnp.dot(a_ref[...], wb_ref[...], preferred_element_type=F32).astype(o_ref.dtype)


def matmul_wcast(a, w, li, tm, tn, out_dtype, name):
    m, k = a.shape
    n = w.shape[2]
    return pl.pallas_call(
        _mm_wcast_kernel,
        grid=(n // tn, m // tm),
        in_specs=[pl.BlockSpec((tm, k), lambda j, i: (i, 0)),
                  pl.BlockSpec((None, k, tn), lambda j, i: (li, 0, j))],
        out_specs=pl.BlockSpec((tm, tn), lambda j, i: (i, j)),
        out_shape=jax.ShapeDtypeStruct((m, n), out_dtype),
        scratch_shapes=[pltpu.VMEM((k, tn), BF16)],
        compiler_params=_params("parallel", "arbitrary"),
        name=name,
    )(a, w)


def _mm_res_norm_kernel(a_ref, w_ref, x_ref, g_ref, *refs, nk, emit_x):
    if emit_x:
        xo_ref, h_ref, acc_ref = refs
    else:
        h_ref, acc_ref = refs
    k = pl.program_id(1)

    @pl.when(k == 0)
    def _():
        acc_ref[...] = x_ref[...]

    acc_ref[...] += jnp.dot(a_ref[...], w_ref[...], preferred_element_type=F32)

    @pl.when(k == nk - 1)
    def _():
        xn = acc_ref[...]
        if emit_x:
            xo_ref[...] = xn
        h_ref[...] = _rms(xn, g_ref[...]).astype(h_ref.dtype)


def matmul_res_norm(a, w, li, x, g, tm, tk, h_dtype, emit_x, name):
    m, kdim = a.shape
    n = w.shape[2]
    nk = kdim // tk
    out_specs = [pl.BlockSpec((tm, n), lambda i, k: (i, 0))]
    out_shape = [jax.ShapeDtypeStruct((m, n), h_dtype)]
    if emit_x:
        out_specs = [pl.BlockSpec((tm, n), lambda i, k: (i, 0))] + out_specs
        out_shape = [jax.ShapeDtypeStruct((m, n), F32)] + out_shape
    return pl.pallas_call(
        functools.partial(_mm_res_norm_kernel, nk=nk, emit_x=emit_x),
        grid=(m // tm, nk),
        in_specs=[pl.BlockSpec((tm, tk), lambda i, k: (i, k)),
                  pl.BlockSpec((None, tk, n), lambda i, k: (li, k, 0)),
                  pl.BlockSpec((tm, n), lambda i, k: (i, 0)),
                  pl.BlockSpec((1, n), lambda i, k: (0, 0))],
        out_specs=out_specs,
        out_shape=out_shape,
        scratch_shapes=[pltpu.VMEM((tm, n), F32)],
        compiler_params=_params("parallel", "arbitrary"),
        name=name,
    )(a, w, x, g.reshape(1, n))


def _mm_cols_res_norm_kernel(a_ref, w_ref, x_ref, g_ref, *refs, nj, tn, emit_x):
    if emit_x:
        xo_ref, h_ref, row_ref = refs
    else:
        h_ref, row_ref = refs
    j = pl.program_id(1)
    row_ref[j] = x_ref[...] + jnp.dot(a_ref[...], w_ref[...], preferred_element_type=F32)

    @pl.when(j == nj - 1)
    def _():
        ssq = jnp.zeros((row_ref.shape[1], 1), F32)
        for jj in range(nj):
            blk = row_ref[jj]
            ssq = ssq + jnp.sum(blk * blk, axis=-1, keepdims=True)
        inv = lax.rsqrt(ssq * (1.0 / (nj * tn)) + EPS)
        for jj in range(nj):
            sl = slice(jj * tn, (jj + 1) * tn)
            blk = row_ref[jj]
            if emit_x:
                xo_ref[:, sl] = blk
            h_ref[:, sl] = (blk * inv * g_ref[:, sl]).astype(h_ref.dtype)


def matmul_cols_res_norm(a, w, li, x, g, tm, tn, h_dtype, emit_x, name):
    m, kdim = a.shape
    n = w.shape[2]
    nj = n // tn
    out_specs = [pl.BlockSpec((tm, n), lambda i, j: (i, 0))]
    out_shape = [jax.ShapeDtypeStruct((m, n), h_dtype)]
    if emit_x:
        out_specs = [pl.BlockSpec((tm, n), lambda i, j: (i, 0))] + out_specs
        out_shape = [jax.ShapeDtypeStruct((m, n), F32)] + out_shape
    return pl.pallas_call(
        functools.partial(_mm_cols_res_norm_kernel, nj=nj, tn=tn, emit_x=emit_x),
        grid=(m // tm, nj),
        in_specs=[pl.BlockSpec((tm, kdim), lambda i, j: (i, 0)),
                  pl.BlockSpec((None, kdim, tn), lambda i, j: (li, 0, j)),
                  pl.BlockSpec((tm, tn), lambda i, j: (i, j)),
                  pl.BlockSpec((1, n), lambda i, j: (0, 0))],
        out_specs=out_specs,
        out_shape=out_shape,
        scratch_shapes=[pltpu.VMEM((nj, tm, tn), F32)],
        compiler_params=_params("parallel", "arbitrary"),
        name=name,
    )(a, w, x, g.reshape(1, n))


CONV_HALO = 32
CONV_RB = 32


def _conv_kernel(a_ref, b_ref, w_ref, bias_ref, lng_ref, lnb_ref, o_ref, ext_ref, y_ref, *, tm):
    nslab = CONV_WIDTH // LANES
    t = pl.program_id(1)

    @pl.when(t == 0)
    def _():
        ext_ref[:, 0:CONV_HALO, :] = jnp.zeros((nslab, CONV_HALO, LANES), F32)

    for c in range(nslab):
        sl = slice(c * LANES, (c + 1) * LANES)
        ext_ref[c, CONV_HALO:CONV_HALO + tm, :] = a_ref[:, sl] * _sigmoid(b_ref[:, sl])

    first = CONV_HALO - (CONV_K - 1)
    for c in range(nslab):
        sl = slice(c * LANES, (c + 1) * LANES)

        def row_block(rb, carry, c=c, sl=sl):
            r0 = pl.multiple_of(rb * CONV_RB, CONV_RB)
            acc = [jnp.broadcast_to(bias_ref[:, sl], (CONV_RB, LANES)), None]
            for k in range(CONV_K):
                term = w_ref[k:k + 1, sl] * ext_ref[c, pl.ds(r0 + first + k, CONV_RB), :]
                acc[k % 2] = term if acc[k % 2] is None else acc[k % 2] + term
            y_ref[pl.ds(r0, CONV_RB), sl] = acc[0] + acc[1]
            return carry

        lax.fori_loop(0, tm // CONV_RB, row_block, 0, unroll=2)

    ext_ref[:, 0:CONV_HALO, :] = ext_ref[:, tm:tm + CONV_HALO, :]

    def ln_block(rb, carry):
        r0 = pl.multiple_of(rb * CONV_RB, CONV_RB)
        y = y_ref[pl.ds(r0, CONV_RB), :]
        mu = jnp.mean(y, axis=-1, keepdims=True)
        yc = y - mu
        yn = yc * lax.rsqrt(jnp.mean(yc * yc, axis=-1, keepdims=True) + EPS)
        yn = yn * lng_ref[...] + lnb_ref[...]
        o_ref[pl.ds(r0, CONV_RB), :] = _silu(yn).astype(o_ref.dtype)
        return carry

    lax.fori_loop(0, tm // CONV_RB, ln_block, 0, unroll=4)


def conv_branch(proj, dw_w, dw_b, ln_g, ln_b, bsz, seq, tm):
    nt = seq // tm
    cw = CONV_WIDTH
    row = lambda b, t: (b * nt + t, 0)
    vec = lambda b, t: (0, 0)
    return pl.pallas_call(
        functools.partial(_conv_kernel, tm=tm),
        grid=(bsz, nt),
        in_specs=[pl.BlockSpec((tm, cw), lambda b, t: (b * nt + t, 0)),
                  pl.BlockSpec((tm, cw), lambda b, t: (b * nt + t, 1)),
                  pl.BlockSpec((CONV_K, cw), vec),
                  pl.BlockSpec((1, cw), vec),
                  pl.BlockSpec((1, cw), vec),
                  pl.BlockSpec((1, cw), vec)],
        out_specs=pl.BlockSpec((tm, cw), row),
        out_shape=jax.ShapeDtypeStruct((bsz * seq, cw), BF16),
        scratch_shapes=[pltpu.VMEM((cw // LANES, CONV_HALO + tm, LANES), F32),
                        pltpu.VMEM((tm, cw), F32)],
        compiler_params=_params("parallel", "arbitrary"),
        name="conv_branch",
    )(proj, proj, dw_w, dw_b.reshape(1, cw), ln_g.reshape(1, cw), ln_b.reshape(1, cw))


SSM_GB = 16
SSM_UB = SSM_GB * SSM_GROUP
SSM_CB = SSM_GB * SSM_STATE
SSM_NJ = 2 * SSM_CB // LANES
SSM_PAD = 4


def _ssm_prep_kernel(ar_ref, ai_ref, ldt_ref, br_ref, bi_ref, abr_ref, abi_ref, bbr_ref, bbi_ref):
    ar = jnp.minimum(ar_ref[...], -1e-4)
    ai = ai_ref[...]
    dt = jnp.exp(ldt_ref[...])
    mag = jnp.exp(dt * ar)
    abar_re = mag * jnp.cos(dt * ai)
    abar_im = mag * jnp.sin(dt * ai)
    den = ar * ar + ai * ai
    nr = abar_re - 1.0
    ni = abar_im
    z_re = (nr * ar + ni * ai) / den
    z_im = (ni * ar - nr * ai) / den
    abr_ref[...] = abar_re
    abi_ref[...] = abar_im
    br = br_ref[...]
    bi = bi_ref[...]
    bbr_ref[...] = z_re * br - z_im * bi
    bbi_ref[...] = z_re * bi + z_im * br


def ssm_prep(a_re, a_im, log_dt, b_re, b_im):
    gp = SSM_GROUPS * SSM_STATE
    ar = a_re.reshape(1, gp)
    ai = a_im.reshape(1, gp)
    ldt = jnp.broadcast_to(log_dt[:, None], (SSM_GROUPS, SSM_STATE)).reshape(1, gp)
    brt = jnp.transpose(b_re, (2, 0, 1)).reshape(SSM_GROUP, gp)
    bit = jnp.transpose(b_im, (2, 0, 1)).reshape(SSM_GROUP, gp)
    vec = jax.ShapeDtypeStruct((1, gp), F32)
    mat = jax.ShapeDtypeStruct((SSM_GROUP, gp), F32)
    return pl.pallas_call(_ssm_prep_kernel, out_shape=[vec, vec, mat, mat], name="ssm_prep")(
        ar, ai, ldt, brt, bit)


def _ssm_kernel(u_ref, bw_ref, cw_ref, abr_ref, abi_ref, d_ref, o_ref, s_ref, st_ref, *, tt, nb, ng):
    pitch = tt + SSM_PAD
    half = SSM_NJ // 2
    chains = [(b, gl) for b in range(nb) for gl in range(ng)]
    t_idx = pl.program_id(1)

    @pl.when(t_idx == 0)
    def _():
        st_ref[...] = jnp.zeros(st_ref.shape, F32)

    for q, (b, gl) in enumerate(chains):
        ucols = slice(gl * SSM_UB, (gl + 1) * SSM_UB)
        bu = jnp.dot(u_ref[b, :, ucols].astype(BF16), bw_ref[gl], preferred_element_type=F32)
        for j in range(SSM_NJ):
            s_ref[q, j * pitch:j * pitch + tt, :] = bu[:, j * LANES:(j + 1) * LANES]

    ar = [abr_ref[gl] for gl in range(ng)]
    ai = [abi_ref[gl] for gl in range(ng)]

    def step(t, carry):
        out = []
        for q, (b, gl) in enumerate(chains):
            xr, xi = carry[2 * q], carry[2 * q + 1]
            sq = s_ref.at[q]
            br = sq[pl.ds(t, half, stride=pitch), :]
            bi = sq[pl.ds(half * pitch + t, half, stride=pitch), :]
            nr = ar[gl] * xr - ai[gl] * xi + br
            ni = ar[gl] * xi + ai[gl] * xr + bi
            sq[pl.ds(t, half, stride=pitch), :] = nr
            sq[pl.ds(half * pitch + t, half, stride=pitch), :] = ni
            out += [nr, ni]
        return tuple(out)

    init = tuple(st_ref[i] for i in range(2 * len(chains)))
    fin = lax.fori_loop(0, tt, step, init, unroll=8)
    for i in range(2 * len(chains)):
        st_ref[i] = fin[i]

    for q, (b, gl) in enumerate(chains):
        ucols = slice(gl * SSM_UB, (gl + 1) * SSM_UB)
        acc = jnp.zeros((tt, SSM_UB), F32)
        for j in range(0, SSM_NJ, 2):
            xj = jnp.concatenate([s_ref[q, j * pitch:j * pitch + tt, :].astype(BF16),
                                  s_ref[q, (j + 1) * pitch:(j + 1) * pitch + tt, :].astype(BF16)], axis=1)
            part = jnp.dot(xj, cw_ref[gl, j * LANES:(j + 2) * LANES, :], preferred_element_type=F32)
            acc = acc + part if j < half else acc - part
        y = acc + d_ref[:, ucols] * u_ref[b, :, ucols]
        o_ref[b, :, ucols] = jax.nn.gelu(y).astype(o_ref.dtype)


def ssm_branch(proj3, bw, cw, abr, abi, d_skip, tt, ng):
    bsz, seq, _ = proj3.shape
    ngb = SSM_WIDTH // SSM_UB
    uw = ng * SSM_UB
    col0 = 2 * CONV_WIDTH // uw
    pitch = tt + SSM_PAD
    return pl.pallas_call(
        functools.partial(_ssm_kernel, tt=tt, nb=bsz, ng=ng),
        grid=(ngb // ng, seq // tt),
        in_specs=[pl.BlockSpec((bsz, tt, uw), lambda g, t: (0, t, col0 + g)),
                  pl.BlockSpec((ng, SSM_UB, 2 * SSM_CB), lambda g, t: (g, 0, 0)),
                  pl.BlockSpec((ng, 2 * SSM_CB, SSM_UB), lambda g, t: (g, 0, 0)),
                  pl.BlockSpec((ng, SUBLANES, LANES), lambda g, t: (g, 0, 0)),
                  pl.BlockSpec((ng, SUBLANES, LANES), lambda g, t: (g, 0, 0)),
                  pl.BlockSpec((1, uw), lambda g, t: (0, g))],
        out_specs=pl.BlockSpec((bsz, tt, uw), lambda g, t: (0, t, g)),
        out_shape=jax.ShapeDtypeStruct((bsz, seq, SSM_WIDTH), BF16),
        scratch_shapes=[pltpu.VMEM((bsz * ng, SSM_NJ * pitch, LANES), F32),
                        pltpu.VMEM((2 * bsz * ng, SUBLANES, LANES), F32)],
        compiler_params=_params("parallel", "arbitrary"),
        name="ssm_branch",
    )(proj3, bw, cw, abr, abi, d_skip.reshape(1, SSM_WIDTH))


def _ssm_block_weights(bbr_t, bbi_t, c_re, c_im):
    ngb = SSM_GROUPS // SSM_GB
    eye = jnp.eye(SSM_GB, dtype=F32)

    def bblock(bt):
        v = bt.reshape(SSM_GROUP, ngb, SSM_GB, SSM_STATE)
        m = jnp.einsum("hbgp,kg->bkhgp", v, eye)
        return m.reshape(ngb, SSM_UB, SSM_CB)

    def cblock(c):
        v = c.reshape(ngb, SSM_GB, SSM_GROUP, SSM_STATE)
        m = jnp.einsum("bghp,kg->bkpgh", v, eye)
        return m.reshape(ngb, SSM_CB, SSM_UB)

    bw = jnp.concatenate([bblock(bbr_t), bblock(bbi_t)], axis=2).astype(BF16)
    cw = jnp.concatenate([cblock(c_re), cblock(c_im)], axis=1).astype(BF16)
    return bw, cw


def _zmix_kernel(hc_ref, ys_ref, wpw_ref, wga_ref, wgb_ref, la_ref, lb_ref, z_ref, pwb_ref, gab_ref, gbb_ref):
    @pl.when(pl.program_id(1) == 0)
    def _():
        pwb_ref[...] = wpw_ref[...].astype(BF16)
        gab_ref[...] = wga_ref[...].astype(BF16)
        gbb_ref[...] = wgb_ref[...].astype(BF16)

    ya = jnp.dot(hc_ref[...], pwb_ref[...], preferred_element_type=F32)
    ys = ys_ref[...]
    ga = jnp.dot(ys, gab_ref[...], preferred_element_type=F32)
    gb = jnp.dot(ys, gbb_ref[...], preferred_element_type=F32)
    yb = ga * _sigmoid(gb)
    z = _sigmoid(la_ref[...]) * ya + _sigmoid(lb_ref[...]) * yb
    z_ref[...] = z.astype(z_ref.dtype)


def zmix(hc, ys, w_pw, w_glu, li, proj, tm, tn):
    m = hc.shape[0]
    nj = D_MODEL // tn
    gate0 = (2 * CONV_WIDTH + SSM_WIDTH) // tn
    return pl.pallas_call(
        _zmix_kernel,
        grid=(nj, m // tm),
        in_specs=[pl.BlockSpec((tm, CONV_WIDTH), lambda j, i: (i, 0)),
                  pl.BlockSpec((tm, SSM_WIDTH), lambda j, i: (i, 0)),
                  pl.BlockSpec((None, CONV_WIDTH, tn), lambda j, i: (li, 0, j)),
                  pl.BlockSpec((None, SSM_WIDTH, tn), lambda j, i: (li, 0, j)),
                  pl.BlockSpec((None, SSM_WIDTH, tn), lambda j, i: (li, 0, nj + j)),
                  pl.BlockSpec((tm, tn), lambda j, i: (i, gate0 + j)),
                  pl.BlockSpec((tm, tn), lambda j, i: (i, gate0 + nj + j))],
        out_specs=pl.BlockSpec((tm, tn), lambda j, i: (i, j)),
        out_shape=jax.ShapeDtypeStruct((m, D_MODEL), BF16),
        scratch_shapes=[pltpu.VMEM((CONV_WIDTH, tn), BF16),
                        pltpu.VMEM((SSM_WIDTH, tn), BF16),
                        pltpu.VMEM((SSM_WIDTH, tn), BF16)],
        compiler_params=_params("parallel", "arbitrary"),
        name="zmix",
    )(hc, ys, w_pw, w_glu, w_glu, proj, proj)


def _attn_kernel(q_ref, k_ref, v_ref, o_ref):
    scale = XA_HEAD_DIM ** -0.5
    for h in range(XA_HEADS):
        sl = slice(h * XA_HEAD_DIM, (h + 1) * XA_HEAD_DIM)
        s = lax.dot_general(q_ref[:, sl], k_ref[:, sl], (((1,), (1,)), ((), ())),
                            preferred_element_type=F32) * scale
        s = s - jnp.max(s, axis=-1, keepdims=True)
        p = jnp.exp(s)
        p = p / jnp.sum(p, axis=-1, keepdims=True)
        o = jnp.dot(p.astype(BF16), v_ref[:, sl], preferred_element_type=F32)
        o_ref[:, sl] = o.astype(o_ref.dtype)


def attention(q, kv, bsz, seq, tm):
    nt = seq // tm
    return pl.pallas_call(
        _attn_kernel,
        grid=(bsz, nt),
        in_specs=[pl.BlockSpec((tm, D_MODEL), lambda b, t: (b * nt + t, 0)),
                  pl.BlockSpec((MEM_LEN, D_MODEL), lambda b, t: (b, 0)),
                  pl.BlockSpec((MEM_LEN, D_MODEL), lambda b, t: (b, 1))],
        out_specs=pl.BlockSpec((tm, D_MODEL), lambda b, t: (b * nt + t, 0)),
        out_shape=jax.ShapeDtypeStruct((bsz * seq, D_MODEL), BF16),
        compiler_params=_params("parallel", "arbitrary"),
        name="attention",
    )(q, kv, kv)


FFN_HALO = 8
FFN_RB = 32


MXU_COLS = 256


def _ffn_up_kernel(h_ref, wg_ref, wv_ref, dg_ref, dv_ref, o_ref, wgb_ref, wvb_ref, eg_ref, ev_ref,
                   *, tm, tn, nt):
    nslab = tn // LANES
    s = pl.program_id(1)

    @pl.when(s == 0)
    def _():
        wgb_ref[...] = wg_ref[...].astype(BF16)
        wvb_ref[...] = wv_ref[...].astype(BF16)

    @pl.when(s % nt == 0)
    def _():
        eg_ref[:, 0:FFN_HALO, :] = jnp.zeros((nslab, FFN_HALO, LANES), F32)
        ev_ref[:, 0:FFN_HALO, :] = jnp.zeros((nslab, FFN_HALO, LANES), F32)

    h = h_ref[...]
    for e_ref, wb_ref in ((eg_ref, wgb_ref), (ev_ref, wvb_ref)):
        for c2 in range(tn // MXU_COLS):
            u = jnp.dot(h, wb_ref[:, c2 * MXU_COLS:(c2 + 1) * MXU_COLS], preferred_element_type=F32)
            for half in range(MXU_COLS // LANES):
                c = c2 * (MXU_COLS // LANES) + half
                e_ref[c, FFN_HALO:FFN_HALO + tm, :] = u[:, half * LANES:(half + 1) * LANES]

    first = FFN_HALO - (FFN_K - 1)
    for c in range(nslab):
        sl = slice(c * LANES, (c + 1) * LANES)
        for rb in range(tm // FFN_RB):
            r0 = rb * FFN_RB
            g = dg_ref[0:1, sl] * eg_ref[c, first + r0:first + r0 + FFN_RB, :]
            v = dv_ref[0:1, sl] * ev_ref[c, first + r0:first + r0 + FFN_RB, :]
            for k in range(1, FFN_K):
                g = g + dg_ref[k:k + 1, sl] * eg_ref[c, first + r0 + k:first + r0 + k + FFN_RB, :]
                v = v + dv_ref[k:k + 1, sl] * ev_ref[c, first + r0 + k:first + r0 + k + FFN_RB, :]
            o_ref[r0:r0 + FFN_RB, sl] = (_silu(g) * v).astype(o_ref.dtype)

    eg_ref[:, 0:FFN_HALO, :] = eg_ref[:, tm:tm + FFN_HALO, :]
    ev_ref[:, 0:FFN_HALO, :] = ev_ref[:, tm:tm + FFN_HALO, :]


def ffn_up(h, w_up, dw_w, li, bsz, seq, tm, tn):
    nt = seq // tm
    nj = D_FF // tn
    slabs = (tn // LANES, FFN_HALO + tm, LANES)
    return pl.pallas_call(
        functools.partial(_ffn_up_kernel, tm=tm, tn=tn, nt=nt),
        grid=(nj, bsz * nt),
        in_specs=[pl.BlockSpec((tm, D_MODEL), lambda j, s: (s, 0)),
                  pl.BlockSpec((None, D_MODEL, tn), lambda j, s: (li, 0, j)),
                  pl.BlockSpec((None, D_MODEL, tn), lambda j, s: (li, 0, nj + j)),
                  pl.BlockSpec((None, FFN_K, tn), lambda j, s: (li, 0, j)),
                  pl.BlockSpec((None, FFN_K, tn), lambda j, s: (li, 0, nj + j))],
        out_specs=pl.BlockSpec((tm, tn), lambda j, s: (s, j)),
        out_shape=jax.ShapeDtypeStruct((bsz * seq, D_FF), BF16),
        scratch_shapes=[pltpu.VMEM((D_MODEL, tn), BF16),
                        pltpu.VMEM((D_MODEL, tn), BF16),
                        pltpu.VMEM(slabs, F32),
                        pltpu.VMEM(slabs, F32)],
        compiler_params=_params("parallel", "arbitrary"),
        name="ffn_up",
    )(h, w_up, w_up, dw_w, dw_w)


def kernel(x, mem, mix_norm_g, w_in, conv_dw_w, conv_dw_b, conv_ln_g, conv_ln_b, conv_w_pw,
           ssm_a_re, ssm_a_im, ssm_log_dt, ssm_b_re, ssm_b_im, ssm_c_re, ssm_c_im, ssm_d,
           ssm_w_glu, w_out, xa_norm_g, mem_norm_g, xa_w_q, xa_w_kv, xa_w_o,
           ffn_norm_g, ffn_w_up, ffn_dw_w, ffn_w_down, final_norm_g):
    bsz, seq, d = x.shape
    depth = w_in.shape[0]
    rows = bsz * seq
    xf = x.reshape(rows, d)
    memf = mem.reshape(bsz * MEM_LEN, d)
    ngb = SSM_GROUPS // SSM_GB

    w_out_b = w_out.astype(BF16)
    w_o_b = xa_w_o.astype(BF16)
    w_down_b = ffn_w_down.astype(BF16)

    h = rmsnorm_rows(xf, mix_norm_g[0], 512, BF16)
    out = None
    for i in range(depth):
        proj = matmul_wcast(h, w_in, i, 1024, 1024, F32, "in_proj")
        hc = conv_branch(proj, conv_dw_w[i], conv_dw_b[i], conv_ln_g[i], conv_ln_b[i], bsz, seq, 512)
        abr, abi, bbr_t, bbi_t = ssm_prep(ssm_a_re[i], ssm_a_im[i], ssm_log_dt[i], ssm_b_re[i], ssm_b_im[i])
        bw, cw = _ssm_block_weights(bbr_t, bbi_t, ssm_c_re[i], ssm_c_im[i])
        ys = ssm_branch(proj.reshape(bsz, seq, -1), bw, cw,
                        abr.reshape(ngb, SUBLANES, LANES), abi.reshape(ngb, SUBLANES, LANES),
                        ssm_d[i], 512, 2).reshape(rows, SSM_WIDTH)
        z = zmix(hc, ys, conv_w_pw, ssm_w_glu, i, proj, 1024, 512)
        xf, h = matmul_res_norm(z, w_out_b, i, xf, xa_norm_g[i], 512, D_MODEL, BF16, True, "out_proj")
        q = matmul_wcast(h, xa_w_q, i, 1024, 1024, BF16, "q_proj")
        mn = rmsnorm_rows(memf, mem_norm_g[i], 256, BF16)
        kv = matmul_wcast(mn, xa_w_kv, i, 512, 1024, BF16, "kv_proj")
        o = attention(q, kv, bsz, seq, 512)
        xf, h = matmul_res_norm(o, w_o_b, i, xf, ffn_norm_g[i], 512, D_MODEL, BF16, True, "o_proj")
        act = ffn_up(h, ffn_w_up, ffn_dw_w, i, bsz, seq, 1024, 512)
        if i + 1 < depth:
            xf, h = matmul_cols_res_norm(act, w_down_b, i, xf, mix_norm_g[i + 1],
                                         512, 512, BF16, True, "down_proj")
        else:
            out = matmul_cols_res_norm(act, w_down_b, i, xf, final_norm_g,
                                       512, 512, F32, False, "down_proj_final")[0]
    return out.reshape(bsz, seq, d)
```

```python
import functools
import math

import jax
import jax.numpy as jnp
from jax import lax
from jax.experimental import pallas as pl
from jax.experimental.pallas import tpu as pltpu

D_MODEL = 2048
MEM_LEN = 256
CONV_WIDTH = 1024
CONV_K = 31
SSM_WIDTH = 1024
SSM_GROUP = 16
SSM_GROUPS = SSM_WIDTH // SSM_GROUP
SSM_STATE = 64
XA_HEADS = 4
XA_HEAD_DIM = D_MODEL // XA_HEADS
D_FF = 5632
FFN_K = 3
EPS = 1e-6

LANES = 128
SUBLANES = 8
MXU_COLS = 256
VMEM_LIMIT = 56 * 1024 * 1024

BF16 = jnp.bfloat16
F32 = jnp.float32


def _params(*sem):
    return pltpu.CompilerParams(dimension_semantics=sem, vmem_limit_bytes=VMEM_LIMIT)


def _rms(x, g):
    return x * lax.rsqrt(jnp.mean(x * x, axis=-1, keepdims=True) + EPS) * g


def _sigmoid(x):
    return 0.5 * jnp.tanh(0.5 * x) + 0.5


def _silu(x):
    hx = 0.5 * x
    return hx * jnp.tanh(hx) + hx


def _rmsnorm_kernel(x_ref, g_ref, o_ref):
    o_ref[...] = _rms(x_ref[...], g_ref[...]).astype(o_ref.dtype)


def rmsnorm_rows(x, g, tm, out_dtype):
    m, d = x.shape
    return pl.pallas_call(
        _rmsnorm_kernel,
        grid=(m // tm,),
        in_specs=[pl.BlockSpec((tm, d), lambda i: (i, 0)),
                  pl.BlockSpec((1, d), lambda i: (0, 0))],
        out_specs=pl.BlockSpec((tm, d), lambda i: (i, 0)),
        out_shape=jax.ShapeDtypeStruct((m, d), out_dtype),
        compiler_params=_params("parallel"),
        name="rmsnorm",
    )(x, g.reshape(1, d))


def _mm_kernel(a_ref, w_ref, o_ref):
    o_ref[...] = jnp.dot(a_ref[...], w_ref[...], preferred_element_type=F32).astype(o_ref.dtype)


def matmul(a, w, li, tm, tn, out_dtype, name):
    m, k = a.shape
    n = w.shape[2]
    return pl.pallas_call(
        _mm_kernel,
        grid=(m // tm, n // tn),
        in_specs=[pl.BlockSpec((tm, k), lambda i, j: (i, 0)),
                  pl.BlockSpec((None, k, tn), lambda i, j: (li, 0, j))],
        out_specs=pl.BlockSpec((tm, tn), lambda i, j: (i, j)),
        out_shape=jax.ShapeDtypeStruct((m, n), out_dtype),
        compiler_params=_params("parallel", "arbitrary"),
        name=name,
    )(a, w)


def _mm_wcast_kernel(a_ref, w_ref, o_ref, wb_ref):
    @pl.when(pl.program_id(1) == 0)
    def _():
        wb_ref[...] = w_ref[...].astype(BF16)

    o_ref[...] = jnp.dot(a_ref[...], wb_ref[...], preferred_element_type=F32).astype(o_ref.dtype)


def matmul_wcast(a, w, li, tm, tn, out_dtype, name):
    m, k = a.shape
    n = w.shape[2]
    return pl.pallas_call(
        _mm_wcast_kernel,
        grid=(n // tn, m // tm),
        in_specs=[pl.BlockSpec((tm, k), lambda j, i: (i, 0)),
                  pl.BlockSpec((None, k, tn), lambda j, i: (li, 0, j))],
        out_specs=pl.BlockSpec((tm, tn), lambda j, i: (i, j)),
        out_shape=jax.ShapeDtypeStruct((m, n), out_dtype),
        scratch_shapes=[pltpu.VMEM((k, tn), BF16)],
        compiler_params=_params("parallel", "arbitrary"),
        name=name,
    )(a, w)


def _mm_res_norm_kernel(a_ref, w_ref, x_ref, g_ref, xo_ref, h_ref, wb_ref):
    @pl.when(pl.program_id(0) == 0)
    def _():
        wb_ref[...] = w_ref[...].astype(BF16)

    xn = x_ref[...] + jnp.dot(a_ref[...], wb_ref[...], preferred_element_type=F32)
    xo_ref[...] = xn
    h_ref[...] = _rms(xn, g_ref[...]).astype(h_ref.dtype)


def matmul_res_norm(a, w, li, x, g, tm, h_dtype, name):
    m, kdim = a.shape
    n = w.shape[2]
    row = lambda i: (i, 0)
    return pl.pallas_call(
        _mm_res_norm_kernel,
        grid=(m // tm,),
        in_specs=[pl.BlockSpec((tm, kdim), row),
                  pl.BlockSpec((None, kdim, n), lambda i: (li, 0, 0), pipeline_mode=pl.Buffered(1)),
                  pl.BlockSpec((tm, n), row),
                  pl.BlockSpec((1, n), lambda i: (0, 0))],
        out_specs=[pl.BlockSpec((tm, n), row), pl.BlockSpec((tm, n), row)],
        out_shape=[jax.ShapeDtypeStruct((m, n), F32), jax.ShapeDtypeStruct((m, n), h_dtype)],
        scratch_shapes=[pltpu.VMEM((kdim, n), BF16)],
        compiler_params=_params("arbitrary"),
        name=name,
    )(a, w, x, g.reshape(1, n))


def _mm_cols_res_norm_kernel(a_ref, w_ref, x_ref, g_ref, *refs, nj, tn, emit_x):
    if emit_x:
        xo_ref, h_ref, row_ref = refs
    else:
        h_ref, row_ref = refs
    j = pl.program_id(1)
    row_ref[j] = x_ref[...] + jnp.dot(a_ref[...], w_ref[...], preferred_element_type=F32)

    @pl.when(j == nj - 1)
    def _():
        ssq = jnp.zeros((row_ref.shape[1], 1), F32)
        for jj in range(nj):
            blk = row_ref[jj]
            ssq = ssq + jnp.sum(blk * blk, axis=-1, keepdims=True)
        inv = lax.rsqrt(ssq * (1.0 / (nj * tn)) + EPS)
        for jj in range(nj):
            sl = slice(jj * tn, (jj + 1) * tn)
            blk = row_ref[jj]
            if emit_x:
                xo_ref[:, sl] = blk
            h_ref[:, sl] = (blk * inv * g_ref[:, sl]).astype(h_ref.dtype)


def matmul_cols_res_norm(a, w, li, x, g, tm, tn, h_dtype, emit_x, name):
    m, kdim = a.shape
    n = w.shape[2]
    nj = n // tn
    out_specs = [pl.BlockSpec((tm, n), lambda i, j: (i, 0))]
    out_shape = [jax.ShapeDtypeStruct((m, n), h_dtype)]
    if emit_x:
        out_specs = [pl.BlockSpec((tm, n), lambda i, j: (i, 0))] + out_specs
        out_shape = [jax.ShapeDtypeStruct((m, n), F32)] + out_shape
    return pl.pallas_call(
        functools.partial(_mm_cols_res_norm_kernel, nj=nj, tn=tn, emit_x=emit_x),
        grid=(m // tm, nj),
        in_specs=[pl.BlockSpec((tm, kdim), lambda i, j: (i, 0)),
                  pl.BlockSpec((None, kdim, tn), lambda i, j: (li, 0, j)),
                  pl.BlockSpec((tm, tn), lambda i, j: (i, j)),
                  pl.BlockSpec((1, n), lambda i, j: (0, 0))],
        out_specs=out_specs,
        out_shape=out_shape,
        scratch_shapes=[pltpu.VMEM((nj, tm, tn), F32)],
        compiler_params=_params("parallel", "arbitrary"),
        name=name,
    )(a, w, x, g.reshape(1, n))


CONV_HALO = 32
CONV_RB = 32


def _conv_kernel(a_ref, b_ref, w_ref, bias_ref, lng_ref, lnb_ref, o_ref, ext_ref, y_ref, *, tm):
    nslab = CONV_WIDTH // LANES
    t = pl.program_id(1)

    @pl.when(t == 0)
    def _():
        ext_ref[:, 0:CONV_HALO, :] = jnp.zeros((nslab, CONV_HALO, LANES), F32)

    for c in range(nslab):
        sl = slice(c * LANES, (c + 1) * LANES)
        ext_ref[c, CONV_HALO:CONV_HALO + tm, :] = a_ref[:, sl] * _sigmoid(b_ref[:, sl])

    first = CONV_HALO - (CONV_K - 1)
    for c in range(nslab):
        sl = slice(c * LANES, (c + 1) * LANES)

        def row_block(rb, carry, c=c, sl=sl):
            r0 = pl.multiple_of(rb * CONV_RB, CONV_RB)
            acc = [jnp.broadcast_to(bias_ref[:, sl], (CONV_RB, LANES)), None]
            for k in range(CONV_K):
                term = w_ref[k:k + 1, sl] * ext_ref[c, pl.ds(r0 + first + k, CONV_RB), :]
                acc[k % 2] = term if acc[k % 2] is None else acc[k % 2] + term
            y_ref[pl.ds(r0, CONV_RB), sl] = acc[0] + acc[1]
            return carry

        lax.fori_loop(0, tm // CONV_RB, row_block, 0, unroll=2)

    ext_ref[:, 0:CONV_HALO, :] = ext_ref[:, tm:tm + CONV_HALO, :]

    def ln_block(rb, carry):
        r0 = pl.multiple_of(rb * CONV_RB, CONV_RB)
        y = y_ref[pl.ds(r0, CONV_RB), :]
        mu = jnp.mean(y, axis=-1, keepdims=True)
        yc = y - mu
        yn = yc * lax.rsqrt(jnp.mean(yc * yc, axis=-1, keepdims=True) + EPS)
        yn = yn * lng_ref[...] + lnb_ref[...]
        o_ref[pl.ds(r0, CONV_RB), :] = _silu(yn).astype(o_ref.dtype)
        return carry

    lax.fori_loop(0, tm // CONV_RB, ln_block, 0, unroll=4)


def conv_branch(proj, dw_w, dw_b, ln_g, ln_b, bsz, seq, tm):
    nt = seq // tm
    cw = CONV_WIDTH
    row = lambda b, t: (b * nt + t, 0)
    vec = lambda b, t: (0, 0)
    return pl.pallas_call(
        functools.partial(_conv_kernel, tm=tm),
        grid=(bsz, nt),
        in_specs=[pl.BlockSpec((tm, cw), lambda b, t: (b * nt + t, 0)),
                  pl.BlockSpec((tm, cw), lambda b, t: (b * nt + t, 1)),
                  pl.BlockSpec((CONV_K, cw), vec),
                  pl.BlockSpec((1, cw), vec),
                  pl.BlockSpec((1, cw), vec),
                  pl.BlockSpec((1, cw), vec)],
        out_specs=pl.BlockSpec((tm, cw), row),
        out_shape=jax.ShapeDtypeStruct((bsz * seq, cw), BF16),
        scratch_shapes=[pltpu.VMEM((cw // LANES, CONV_HALO + tm, LANES), F32),
                        pltpu.VMEM((tm, cw), F32)],
        compiler_params=_params("parallel", "arbitrary"),
        name="conv_branch",
    )(proj, proj, dw_w, dw_b.reshape(1, cw), ln_g.reshape(1, cw), ln_b.reshape(1, cw))


SSM_GB = 16
SSM_UB = SSM_GB * SSM_GROUP
SSM_CB = SSM_GB * SSM_STATE
SSM_NJ = 2 * SSM_CB // LANES
SSM_PAD = 4


def _ssm_prep_kernel(ar_ref, ai_ref, ldt_ref, br_ref, bi_ref, abr_ref, abi_ref, bbr_ref, bbi_ref):
    ar = jnp.minimum(ar_ref[...], -1e-4)
    ai = ai_ref[...]
    dt = jnp.exp(ldt_ref[...])
    mag = jnp.exp(dt * ar)
    abar_re = mag * jnp.cos(dt * ai)
    abar_im = mag * jnp.sin(dt * ai)
    den = ar * ar + ai * ai
    nr = abar_re - 1.0
    ni = abar_im
    z_re = (nr * ar + ni * ai) / den
    z_im = (ni * ar - nr * ai) / den
    abr_ref[...] = abar_re
    abi_ref[...] = abar_im
    br = br_ref[...]
    bi = bi_ref[...]
    bbr_ref[...] = z_re * br - z_im * bi
    bbi_ref[...] = z_re * bi + z_im * br


def ssm_prep(a_re, a_im, log_dt, b_re, b_im):
    gp = SSM_GROUPS * SSM_STATE
    ar = a_re.reshape(1, gp)
    ai = a_im.reshape(1, gp)
    ldt = jnp.broadcast_to(log_dt[:, None], (SSM_GROUPS, SSM_STATE)).reshape(1, gp)
    brt = jnp.transpose(b_re, (2, 0, 1)).reshape(SSM_GROUP, gp)
    bit = jnp.transpose(b_im, (2, 0, 1)).reshape(SSM_GROUP, gp)
    vec = jax.ShapeDtypeStruct((1, gp), F32)
    mat = jax.ShapeDtypeStruct((SSM_GROUP, gp), F32)
    return pl.pallas_call(_ssm_prep_kernel, out_shape=[vec, vec, mat, mat], name="ssm_prep")(
        ar, ai, ldt, brt, bit)


def _ssm_kernel(u_ref, bw_ref, cw_ref, abr_ref, abi_ref, d_ref, o_ref, s_ref, st_ref, *, tt, nb, ng):
    pitch = tt + SSM_PAD
    half = SSM_NJ // 2
    chains = [(b, gl) for b in range(nb) for gl in range(ng)]
    t_idx = pl.program_id(1)

    @pl.when(t_idx == 0)
    def _():
        st_ref[...] = jnp.zeros(st_ref.shape, F32)

    for q, (b, gl) in enumerate(chains):
        ucols = slice(gl * SSM_UB, (gl + 1) * SSM_UB)
        bu = jnp.dot(u_ref[b, :, ucols].astype(BF16), bw_ref[gl], preferred_element_type=F32)
        for j in range(SSM_NJ):
            s_ref[q, j * pitch:j * pitch + tt, :] = bu[:, j * LANES:(j + 1) * LANES]

    ar = [abr_ref[gl] for gl in range(ng)]
    ai = [abi_ref[gl] for gl in range(ng)]

    def step(t, carry):
        out = []
        for q, (b, gl) in enumerate(chains):
            xr, xi = carry[2 * q], carry[2 * q + 1]
            sq = s_ref.at[q]
            br = sq[pl.ds(t, half, stride=pitch), :]
            bi = sq[pl.ds(half * pitch + t, half, stride=pitch), :]
            nr = ar[gl] * xr - ai[gl] * xi + br
            ni = ar[gl] * xi + ai[gl] * xr + bi
            sq[pl.ds(t, half, stride=pitch), :] = nr
            sq[pl.ds(half * pitch + t, half, stride=pitch), :] = ni
            out += [nr, ni]
        return tuple(out)

    init = tuple(st_ref[i] for i in range(2 * len(chains)))
    fin = lax.fori_loop(0, tt, step, init, unroll=8)
    for i in range(2 * len(chains)):
        st_ref[i] = fin[i]

    for q, (b, gl) in enumerate(chains):
        ucols = slice(gl * SSM_UB, (gl + 1) * SSM_UB)
        acc = jnp.zeros((tt, SSM_UB), F32)
        for j in range(0, SSM_NJ, 2):
            xj = jnp.concatenate([s_ref[q, j * pitch:j * pitch + tt, :].astype(BF16),
                                  s_ref[q, (j + 1) * pitch:(j + 1) * pitch + tt, :].astype(BF16)], axis=1)
            part = jnp.dot(xj, cw_ref[gl, j * LANES:(j + 2) * LANES, :], preferred_element_type=F32)
            acc = acc + part if j < half else acc - part
        y = acc + d_ref[:, ucols] * u_ref[b, :, ucols]
        o_ref[b, :, ucols] = jax.nn.gelu(y).astype(o_ref.dtype)


def ssm_branch(proj3, bw, cw, abr, abi, d_skip, tt, ng):
    bsz, seq, _ = proj3.shape
    ngb = SSM_WIDTH // SSM_UB
    uw = ng * SSM_UB
    col0 = 2 * CONV_WIDTH // uw
    pitch = tt + SSM_PAD
    return pl.pallas_call(
        functools.partial(_ssm_kernel, tt=tt, nb=bsz, ng=ng),
        grid=(ngb // ng, seq // tt),
        in_specs=[pl.BlockSpec((bsz, tt, uw), lambda g, t: (0, t, col0 + g)),
                  pl.BlockSpec((ng, SSM_UB, 2 * SSM_CB), lambda g, t: (g, 0, 0)),
                  pl.BlockSpec((ng, 2 * SSM_CB, SSM_UB), lambda g, t: (g, 0, 0)),
                  pl.BlockSpec((ng, SUBLANES, LANES), lambda g, t: (g, 0, 0)),
                  pl.BlockSpec((ng, SUBLANES, LANES), lambda g, t: (g, 0, 0)),
                  pl.BlockSpec((1, uw), lambda g, t: (0, g))],
        out_specs=pl.BlockSpec((bsz, tt, uw), lambda g, t: (0, t, g)),
        out_shape=jax.ShapeDtypeStruct((bsz, seq, SSM_WIDTH), BF16),
        scratch_shapes=[pltpu.VMEM((bsz * ng, SSM_NJ * pitch, LANES), F32),
                        pltpu.VMEM((2 * bsz * ng, SUBLANES, LANES), F32)],
        compiler_params=_params("parallel", "arbitrary"),
        name="ssm_branch",
    )(proj3, bw, cw, abr, abi, d_skip.reshape(1, SSM_WIDTH))


def _ssm_block_weights(bbr_t, bbi_t, c_re, c_im):
    ngb = SSM_GROUPS // SSM_GB
    eye = jnp.eye(SSM_GB, dtype=F32)

    def bblock(bt):
        v = bt.reshape(SSM_GROUP, ngb, SSM_GB, SSM_STATE)
        m = jnp.einsum("hbgp,kg->bkhgp", v, eye)
        return m.reshape(ngb, SSM_UB, SSM_CB)

    def cblock(c):
        v = c.reshape(ngb, SSM_GB, SSM_GROUP, SSM_STATE)
        m = jnp.einsum("bghp,kg->bkpgh", v, eye)
        return m.reshape(ngb, SSM_CB, SSM_UB)

    bw = jnp.concatenate([bblock(bbr_t), bblock(bbi_t)], axis=2).astype(BF16)
    cw = jnp.concatenate([cblock(c_re), cblock(c_im)], axis=1).astype(BF16)
    return bw, cw


def _zmix_kernel(hc_ref, ys_ref, wpw_ref, wga_ref, wgb_ref, la_ref, lb_ref, z_ref, pwb_ref, gab_ref, gbb_ref):
    @pl.when(pl.program_id(1) == 0)
    def _():
        pwb_ref[...] = wpw_ref[...].astype(BF16)
        gab_ref[...] = wga_ref[...].astype(BF16)
        gbb_ref[...] = wgb_ref[...].astype(BF16)

    hc = hc_ref[...]
    ys = ys_ref[...]
    for c in range(z_ref.shape[1] // MXU_COLS):
        sl = slice(c * MXU_COLS, (c + 1) * MXU_COLS)
        ya = jnp.dot(hc, pwb_ref[:, sl], preferred_element_type=F32)
        ga = jnp.dot(ys, gab_ref[:, sl], preferred_element_type=F32)
        gb = jnp.dot(ys, gbb_ref[:, sl], preferred_element_type=F32)
        yb = ga * _sigmoid(gb)
        z = _sigmoid(la_ref[:, sl]) * ya + _sigmoid(lb_ref[:, sl]) * yb
        z_ref[:, sl] = z.astype(z_ref.dtype)


def zmix(hc, ys, w_pw, w_glu, li, proj, tm, tn):
    m = hc.shape[0]
    nj = D_MODEL // tn
    gate0 = (2 * CONV_WIDTH + SSM_WIDTH) // tn
    return pl.pallas_call(
        _zmix_kernel,
        grid=(nj, m // tm),
        in_specs=[pl.BlockSpec((tm, CONV_WIDTH), lambda j, i: (i, 0)),
                  pl.BlockSpec((tm, SSM_WIDTH), lambda j, i: (i, 0)),
                  pl.BlockSpec((None, CONV_WIDTH, tn), lambda j, i: (li, 0, j)),
                  pl.BlockSpec((None, SSM_WIDTH, tn), lambda j, i: (li, 0, j)),
                  pl.BlockSpec((None, SSM_WIDTH, tn), lambda j, i: (li, 0, nj + j)),
                  pl.BlockSpec((tm, tn), lambda j, i: (i, gate0 + j)),
                  pl.BlockSpec((tm, tn), lambda j, i: (i, gate0 + nj + j))],
        out_specs=pl.BlockSpec((tm, tn), lambda j, i: (i, j)),
        out_shape=jax.ShapeDtypeStruct((m, D_MODEL), BF16),
        scratch_shapes=[pltpu.VMEM((CONV_WIDTH, tn), BF16),
                        pltpu.VMEM((SSM_WIDTH, tn), BF16),
                        pltpu.VMEM((SSM_WIDTH, tn), BF16)],
        compiler_params=_params("parallel", "arbitrary"),
        name="zmix",
    )(hc, ys, w_pw, w_glu, w_glu, proj, proj)


def _attn_kernel(q_ref, k_ref, v_ref, o_ref):
    scale = XA_HEAD_DIM ** -0.5
    for h in range(XA_HEADS):
        sl = slice(h * XA_HEAD_DIM, (h + 1) * XA_HEAD_DIM)
        s = lax.dot_general(q_ref[:, sl], k_ref[:, sl], (((1,), (1,)), ((), ())),
                            preferred_element_type=F32) * scale
        s = s - jnp.max(s, axis=-1, keepdims=True)
        p = jnp.exp(s)
        p = p / jnp.sum(p, axis=-1, keepdims=True)
        o = jnp.dot(p.astype(BF16), v_ref[:, sl], preferred_element_type=F32)
        o_ref[:, sl] = o.astype(o_ref.dtype)


def attention(q, kv, bsz, seq, tm):
    nt = seq // tm
    return pl.pallas_call(
        _attn_kernel,
        grid=(bsz, nt),
        in_specs=[pl.BlockSpec((tm, D_MODEL), lambda b, t: (b * nt + t, 0)),
                  pl.BlockSpec((MEM_LEN, D_MODEL), lambda b, t: (b, 0)),
                  pl.BlockSpec((MEM_LEN, D_MODEL), lambda b, t: (b, 1))],
        out_specs=pl.BlockSpec((tm, D_MODEL), lambda b, t: (b * nt + t, 0)),
        out_shape=jax.ShapeDtypeStruct((bsz * seq, D_MODEL), BF16),
        compiler_params=_params("parallel", "arbitrary"),
        name="attention",
    )(q, kv, kv)


FFN_HALO = 8
FFN_RB = 32


def _ffn_up_kernel(h_ref, wg_ref, wv_ref, dg_ref, dv_ref, o_ref, wgb_ref, wvb_ref, eg_ref, ev_ref,
                   *, tm, tn, nt):
    nslab = tn // LANES
    s = pl.program_id(1)

    @pl.when(s == 0)
    def _():
        wgb_ref[...] = wg_ref[...].astype(BF16)
        wvb_ref[...] = wv_ref[...].astype(BF16)

    @pl.when(s % nt == 0)
    def _():
        eg_ref[:, 0:FFN_HALO, :] = jnp.zeros((nslab, FFN_HALO, LANES), F32)
        ev_ref[:, 0:FFN_HALO, :] = jnp.zeros((nslab, FFN_HALO, LANES), F32)

    h = h_ref[...]
    for e_ref, wb_ref in ((eg_ref, wgb_ref), (ev_ref, wvb_ref)):
        for c2 in range(tn // MXU_COLS):
            u = jnp.dot(h, wb_ref[:, c2 * MXU_COLS:(c2 + 1) * MXU_COLS], preferred_element_type=F32)
            for half in range(MXU_COLS // LANES):
                c = c2 * (MXU_COLS // LANES) + half
                e_ref[c, FFN_HALO:FFN_HALO + tm, :] = u[:, half * LANES:(half + 1) * LANES]

    first = FFN_HALO - (FFN_K - 1)
    for c in range(nslab):
        sl = slice(c * LANES, (c + 1) * LANES)
        for rb in range(tm // FFN_RB):
            r0 = rb * FFN_RB
            g = dg_ref[0:1, sl] * eg_ref[c, first + r0:first + r0 + FFN_RB, :]
            v = dv_ref[0:1, sl] * ev_ref[c, first + r0:first + r0 + FFN_RB, :]
            for k in range(1, FFN_K):
                g = g + dg_ref[k:k + 1, sl] * eg_ref[c, first + r0 + k:first + r0 + k + FFN_RB, :]
                v = v + dv_ref[k:k + 1, sl] * ev_ref[c, first + r0 + k:first + r0 + k + FFN_RB, :]
            o_ref[r0:r0 + FFN_RB, sl] = (_silu(g) * v).astype(o_ref.dtype)

    eg_ref[:, 0:FFN_HALO, :] = eg_ref[:, tm:tm + FFN_HALO, :]
    ev_ref[:, 0:FFN_HALO, :] = ev_ref[:, tm:tm + FFN_HALO, :]


def ffn_up(h, w_up, dw_w, li, bsz, seq, tm, tn):
    nt = seq // tm
    nj = D_FF // tn
    slabs = (tn // LANES, FFN_HALO + tm, LANES)
    return pl.pallas_call(
        functools.partial(_ffn_up_kernel, tm=tm, tn=tn, nt=nt),
        grid=(nj, bsz * nt),
        in_specs=[pl.BlockSpec((tm, D_MODEL), lambda j, s: (s, 0)),
                  pl.BlockSpec((None, D_MODEL, tn), lambda j, s: (li, 0, j)),
                  pl.BlockSpec((None, D_MODEL, tn), lambda j, s: (li, 0, nj + j)),
                  pl.BlockSpec((None, FFN_K, tn), lambda j, s: (li, 0, j)),
                  pl.BlockSpec((None, FFN_K, tn), lambda j, s: (li, 0, nj + j))],
        out_specs=pl.BlockSpec((tm, tn), lambda j, s: (s, j)),
        out_shape=jax.ShapeDtypeStruct((bsz * seq, D_FF), BF16),
        scratch_shapes=[pltpu.VMEM((D_MODEL, tn), BF16),
                        pltpu.VMEM((D_MODEL, tn), BF16),
                        pltpu.VMEM(slabs, F32),
                        pltpu.VMEM(slabs, F32)],
        compiler_params=_params("parallel", "arbitrary"),
        name="ffn_up",
    )(h, w_up, w_up, dw_w, dw_w)


def kernel(x, mem, mix_norm_g, w_in, conv_dw_w, conv_dw_b, conv_ln_g, conv_ln_b, conv_w_pw,
           ssm_a_re, ssm_a_im, ssm_log_dt, ssm_b_re, ssm_b_im, ssm_c_re, ssm_c_im, ssm_d,
           ssm_w_glu, w_out, xa_norm_g, mem_norm_g, xa_w_q, xa_w_kv, xa_w_o,
           ffn_norm_g, ffn_w_up, ffn_dw_w, ffn_w_down, final_norm_g):
    bsz, seq, d = x.shape
    depth = w_in.shape[0]
    rows = bsz * seq
    xf = x.reshape(rows, d)
    memf = mem.reshape(bsz * MEM_LEN, d)
    ngb = SSM_GROUPS // SSM_GB

    w_down_b = ffn_w_down.astype(BF16)

    h = rmsnorm_rows(xf, mix_norm_g[0], 512, BF16)
    out = None
    for i in range(depth):
        proj = matmul_wcast(h, w_in, i, 2048, 512, F32, "in_proj")
        hc = conv_branch(proj, conv_dw_w[i], conv_dw_b[i], conv_ln_g[i], conv_ln_b[i], bsz, seq, 512)
        abr, abi, bbr_t, bbi_t = ssm_prep(ssm_a_re[i], ssm_a_im[i], ssm_log_dt[i], ssm_b_re[i], ssm_b_im[i])
        bw, cw = _ssm_block_weights(bbr_t, bbi_t, ssm_c_re[i], ssm_c_im[i])
        ys = ssm_branch(proj.reshape(bsz, seq, -1), bw, cw,
                        abr.reshape(ngb, SUBLANES, LANES), abi.reshape(ngb, SUBLANES, LANES),
                        ssm_d[i], 512, 2).reshape(rows, SSM_WIDTH)
        z = zmix(hc, ys, conv_w_pw, ssm_w_glu, i, proj, 1024, 512)
        xf, h = matmul_res_norm(z, w_out, i, xf, xa_norm_g[i], 512, BF16, "out_proj")
        q = matmul_wcast(h, xa_w_q, i, 2048, 512, BF16, "q_proj")
        mn = rmsnorm_rows(memf, mem_norm_g[i], 256, BF16)
        kv = matmul_wcast(mn, xa_w_kv, i, 512, 1024, BF16, "kv_proj")
        o = attention(q, kv, bsz, seq, 512)
        xf, h = matmul_res_norm(o, xa_w_o, i, xf, ffn_norm_g[i], 512, BF16, "o_proj")
        act = ffn_up(h, ffn_w_up, ffn_dw_w, i, bsz, seq, 1024, 512)
        if i + 1 < depth:
            xf, h = matmul_cols_res_norm(act, w_down_b, i, xf, mix_norm_g[i + 1],
                                         512, 512, BF16, True, "down_proj")
        else:
            out = matmul_cols_res_norm(act, w_down_b, i, xf, final_norm_g,
                                       512, 512, F32, False, "down_proj_final")[0]
    return out.reshape(bsz, seq, d)
```

```python
import functools
import math

import jax
import jax.numpy as jnp
from jax import lax
from jax.experimental import pallas as pl
from jax.experimental.pallas import tpu as pltpu

D_MODEL = 2048
MEM_LEN = 256
CONV_WIDTH = 1024
CONV_K = 31
SSM_WIDTH = 1024
SSM_GROUP = 16
SSM_GROUPS = SSM_WIDTH // SSM_GROUP
SSM_STATE = 64
XA_HEADS = 4
XA_HEAD_DIM = D_MODEL // XA_HEADS
D_FF = 5632
FFN_K = 3
EPS = 1e-6

LANES = 128
SUBLANES = 8
MXU_COLS = 256
VMEM_LIMIT = 56 * 1024 * 1024

BF16 = jnp.bfloat16
F32 = jnp.float32


def _params(*sem):
    return pltpu.CompilerParams(dimension_semantics=sem, vmem_limit_bytes=VMEM_LIMIT)


def _rms(x, g):
    return x * lax.rsqrt(jnp.mean(x * x, axis=-1, keepdims=True) + EPS) * g


def _sigmoid(x):
    return 0.5 * jnp.tanh(0.5 * x) + 0.5


def _silu(x):
    hx = 0.5 * x
    return hx * jnp.tanh(hx) + hx


def _rmsnorm_kernel(x_ref, g_ref, o_ref):
    o_ref[...] = _rms(x_ref[...], g_ref[...]).astype(o_ref.dtype)


def rmsnorm_rows(x, g, tm, out_dtype):
    m, d = x.shape
    return pl.pallas_call(
        _rmsnorm_kernel,
        grid=(m // tm,),
        in_specs=[pl.BlockSpec((tm, d), lambda i: (i, 0)),
                  pl.BlockSpec((1, d), lambda i: (0, 0))],
        out_specs=pl.BlockSpec((tm, d), lambda i: (i, 0)),
        out_shape=jax.ShapeDtypeStruct((m, d), out_dtype),
        compiler_params=_params("parallel"),
        name="rmsnorm",
    )(x, g.reshape(1, d))


def _mm_kernel(a_ref, w_ref, o_ref):
    o_ref[...] = jnp.dot(a_ref[...], w_ref[...], preferred_element_type=F32).astype(o_ref.dtype)


def matmul(a, w, li, tm, tn, out_dtype, name):
    m, k = a.shape
    n = w.shape[2]
    return pl.pallas_call(
        _mm_kernel,
        grid=(m // tm, n // tn),
        in_specs=[pl.BlockSpec((tm, k), lambda i, j: (i, 0)),
                  pl.BlockSpec((None, k, tn), lambda i, j: (li, 0, j))],
        out_specs=pl.BlockSpec((tm, tn), lambda i, j: (i, j)),
        out_shape=jax.ShapeDtypeStruct((m, n), out_dtype),
        compiler_params=_params("parallel", "arbitrary"),
        name=name,
    )(a, w)


def _mm_wcast_kernel(a_ref, w_ref, o_ref, wb_ref):
    @pl.when(pl.program_id(1) == 0)
    def _():
        wb_ref[...] = w_ref[...].astype(BF16)

    o_ref[...] = jnp.dot(a_ref[...], wb_ref[...], preferred_element_type=F32).astype(o_ref.dtype)


def matmul_wcast(a, w, li, tm, tn, out_dtype, name):
    m, k = a.shape
    n = w.shape[2]
    return pl.pallas_call(
        _mm_wcast_kernel,
        grid=(n // tn, m // tm),
        in_specs=[pl.BlockSpec((tm, k), lambda j, i: (i, 0)),
                  pl.BlockSpec((None, k, tn), lambda j, i: (li, 0, j))],
        out_specs=pl.BlockSpec((tm, tn), lambda j, i: (i, j)),
        out_shape=jax.ShapeDtypeStruct((m, n), out_dtype),
        scratch_shapes=[pltpu.VMEM((k, tn), BF16)],
        compiler_params=_params("parallel", "arbitrary"),
        name=name,
    )(a, w)


def _mm_res_norm_kernel(a_ref, w_ref, x_ref, g_ref, xo_ref, h_ref, wb_ref):
    @pl.when(pl.program_id(0) == 0)
    def _():
        wb_ref[...] = w_ref[...].astype(BF16)

    xn = x_ref[...] + jnp.dot(a_ref[...], wb_ref[...], preferred_element_type=F32)
    xo_ref[...] = xn
    h_ref[...] = _rms(xn, g_ref[...]).astype(h_ref.dtype)


def matmul_res_norm(a, w, li, x, g, tm, h_dtype, name):
    m, kdim = a.shape
    n = w.shape[2]
    row = lambda i: (i, 0)
    return pl.pallas_call(
        _mm_res_norm_kernel,
        grid=(m // tm,),
        in_specs=[pl.BlockSpec((tm, kdim), row),
                  pl.BlockSpec((None, kdim, n), lambda i: (li, 0, 0), pipeline_mode=pl.Buffered(1)),
                  pl.BlockSpec((tm, n), row),
                  pl.BlockSpec((1, n), lambda i: (0, 0))],
        out_specs=[pl.BlockSpec((tm, n), row), pl.BlockSpec((tm, n), row)],
        out_shape=[jax.ShapeDtypeStruct((m, n), F32), jax.ShapeDtypeStruct((m, n), h_dtype)],
        scratch_shapes=[pltpu.VMEM((kdim, n), BF16)],
        compiler_params=_params("arbitrary"),
        name=name,
    )(a, w, x, g.reshape(1, n))


def _mm_cols_res_norm_kernel(a_ref, w_ref, x_ref, g_ref, *refs, nk, nj, tn, emit_x):
    if emit_x:
        xo_ref, h_ref, row_ref = refs
    else:
        h_ref, row_ref = refs
    k = pl.program_id(1)
    j = pl.program_id(2)

    @pl.when(k == 0)
    def _():
        row_ref[j] = x_ref[...]

    row_ref[j] += jnp.dot(a_ref[...], w_ref[...], preferred_element_type=F32)

    if emit_x:
        @pl.when(k == nk - 1)
        def _():
            xo_ref[...] = row_ref[j]

    @pl.when((k == nk - 1) & (j == nj - 1))
    def _():
        ssq = jnp.zeros((row_ref.shape[1], 1), F32)
        for jj in range(nj):
            blk = row_ref[jj]
            ssq = ssq + jnp.sum(blk * blk, axis=-1, keepdims=True)
        inv = lax.rsqrt(ssq * (1.0 / (nj * tn)) + EPS)
        for jj in range(nj):
            sl = slice(jj * tn, (jj + 1) * tn)
            h_ref[:, sl] = (row_ref[jj] * inv * g_ref[:, sl]).astype(h_ref.dtype)


def matmul_cols_res_norm(a, w, li, x, g, tm, tn, nk, h_dtype, emit_x, name):
    m, kdim = a.shape
    n = w.shape[2]
    nj = n // tn
    tk = kdim // nk
    last = nk - 1
    out_specs = [pl.BlockSpec((tm, n), lambda i, k, j: (i, 0))]
    out_shape = [jax.ShapeDtypeStruct((m, n), h_dtype)]
    if emit_x:
        out_specs = [pl.BlockSpec((tm, tn), lambda i, k, j: (i, jnp.where(k == last, j, 0)))] + out_specs
        out_shape = [jax.ShapeDtypeStruct((m, n), F32)] + out_shape
    return pl.pallas_call(
        functools.partial(_mm_cols_res_norm_kernel, nk=nk, nj=nj, tn=tn, emit_x=emit_x),
        grid=(m // tm, nk, nj),
        in_specs=[pl.BlockSpec((tm, tk), lambda i, k, j: (i, k)),
                  pl.BlockSpec((None, tk, tn), lambda i, k, j: (li, k, j)),
                  pl.BlockSpec((tm, tn), lambda i, k, j: (i, jnp.where(k == 0, j, nj - 1))),
                  pl.BlockSpec((1, n), lambda i, k, j: (0, 0))],
        out_specs=out_specs,
        out_shape=out_shape,
        scratch_shapes=[pltpu.VMEM((nj, tm, tn), F32)],
        compiler_params=_params("parallel", "arbitrary", "arbitrary"),
        name=name,
    )(a, w, x, g.reshape(1, n))


CONV_HALO = 32
CONV_RB = 32


def _conv_kernel(a_ref, b_ref, w_ref, bias_ref, lng_ref, lnb_ref, o_ref, ext_ref, y_ref, *, tm):
    nslab = CONV_WIDTH // LANES
    t = pl.program_id(1)

    @pl.when(t == 0)
    def _():
        ext_ref[:, 0:CONV_HALO, :] = jnp.zeros((nslab, CONV_HALO, LANES), F32)

    for c in range(nslab):
        sl = slice(c * LANES, (c + 1) * LANES)
        ext_ref[c, CONV_HALO:CONV_HALO + tm, :] = a_ref[:, sl] * _sigmoid(b_ref[:, sl])

    first = CONV_HALO - (CONV_K - 1)
    for c in range(nslab):
        sl = slice(c * LANES, (c + 1) * LANES)

        def row_block(rb, carry, c=c, sl=sl):
            r0 = pl.multiple_of(rb * CONV_RB, CONV_RB)
            acc = [jnp.broadcast_to(bias_ref[:, sl], (CONV_RB, LANES)), None]
            for k in range(CONV_K):
                term = w_ref[k:k + 1, sl] * ext_ref[c, pl.ds(r0 + first + k, CONV_RB), :]
                acc[k % 2] = term if acc[k % 2] is None else acc[k % 2] + term
            y_ref[pl.ds(r0, CONV_RB), sl] = acc[0] + acc[1]
            return carry

        lax.fori_loop(0, tm // CONV_RB, row_block, 0, unroll=2)

    ext_ref[:, 0:CONV_HALO, :] = ext_ref[:, tm:tm + CONV_HALO, :]

    def ln_block(rb, carry):
        r0 = pl.multiple_of(rb * CONV_RB, CONV_RB)
        y = y_ref[pl.ds(r0, CONV_RB), :]
        mu = jnp.mean(y, axis=-1, keepdims=True)
        yc = y - mu
        yn = yc * lax.rsqrt(jnp.mean(yc * yc, axis=-1, keepdims=True) + EPS)
        yn = yn * lng_ref[...] + lnb_ref[...]
        o_ref[pl.ds(r0, CONV_RB), :] = _silu(yn).astype(o_ref.dtype)
        return carry

    lax.fori_loop(0, tm // CONV_RB, ln_block, 0, unroll=4)


def conv_branch(proj, dw_w, dw_b, ln_g, ln_b, bsz, seq, tm):
    nt = seq // tm
    cw = CONV_WIDTH
    row = lambda b, t: (b * nt + t, 0)
    vec = lambda b, t: (0, 0)
    return pl.pallas_call(
        functools.partial(_conv_kernel, tm=tm),
        grid=(bsz, nt),
        in_specs=[pl.BlockSpec((tm, cw), lambda b, t: (b * nt + t, 0)),
                  pl.BlockSpec((tm, cw), lambda b, t: (b * nt + t, 1)),
                  pl.BlockSpec((CONV_K, cw), vec),
                  pl.BlockSpec((1, cw), vec),
                  pl.BlockSpec((1, cw), vec),
                  pl.BlockSpec((1, cw), vec)],
        out_specs=pl.BlockSpec((tm, cw), row),
        out_shape=jax.ShapeDtypeStruct((bsz * seq, cw), BF16),
        scratch_shapes=[pltpu.VMEM((cw // LANES, CONV_HALO + tm, LANES), F32),
                        pltpu.VMEM((tm, cw), F32)],
        compiler_params=_params("parallel", "arbitrary"),
        name="conv_branch",
    )(proj, proj, dw_w, dw_b.reshape(1, cw), ln_g.reshape(1, cw), ln_b.reshape(1, cw))


SSM_GB = 16
SSM_UB = SSM_GB * SSM_GROUP
SSM_CB = SSM_GB * SSM_STATE
SSM_NJ = 2 * SSM_CB // LANES
SSM_PAD = 4


def _zoh(ar, ai, ldt):
    ar = jnp.minimum(ar, -1e-4)
    dt = jnp.exp(ldt)
    mag = jnp.exp(dt * ar)
    return mag * jnp.cos(dt * ai), mag * jnp.sin(dt * ai), ar


def _ssm_prep_kernel(arow_ref, avec_ref, bt_ref, ct_ref, abar_ref, bw_ref, cw_ref):
    are, aim, _ = _zoh(avec_ref[0], avec_ref[1], avec_ref[2])
    abar_ref[0] = are
    abar_ref[1] = aim

    abar_re, abar_im, ar = _zoh(arow_ref[0], arow_ref[1], arow_ref[2])
    ai = arow_ref[1]
    den = ar * ar + ai * ai
    nr = abar_re - 1.0
    z_re = (nr * ar + abar_im * ai) / den
    z_im = (abar_im * ar - nr * ai) / den
    br = bt_ref[0]
    bi = bt_ref[1]
    bbar = (z_re * br - z_im * bi, z_re * bi + z_im * br)

    rg = lax.broadcasted_iota(jnp.int32, (SSM_UB, SSM_CB), 0) // SSM_GROUP
    cg = lax.broadcasted_iota(jnp.int32, (SSM_UB, SSM_CB), 1) // SSM_STATE
    for part in range(2):
        tiled = jnp.concatenate([bbar[part]] * SSM_GB, axis=0)
        bw_ref[:, part * SSM_CB:(part + 1) * SSM_CB] = jnp.where(rg == cg, tiled, 0.0).astype(BF16)

    rg = lax.broadcasted_iota(jnp.int32, (SSM_CB, SSM_UB), 0) // SSM_STATE
    cg = lax.broadcasted_iota(jnp.int32, (SSM_CB, SSM_UB), 1) // SSM_GROUP
    for part in range(2):
        tiled = jnp.concatenate([ct_ref[part]] * SSM_GB, axis=0)
        cw_ref[part * SSM_CB:(part + 1) * SSM_CB, :] = jnp.where(rg == cg, tiled, 0.0).astype(BF16)


def ssm_prep(a_re, a_im, log_dt, b_re, b_im, c_re, c_im):
    nl = a_re.shape[0]
    ngb = SSM_GROUPS // SSM_GB
    nblk = nl * ngb
    ldt = jnp.broadcast_to(log_dt[:, :, None], a_re.shape)
    a3 = jnp.stack([a_re, a_im, ldt], axis=1).reshape(nl, 3, ngb, SSM_CB)
    a3 = jnp.transpose(a3, (0, 2, 1, 3)).reshape(nblk, 3, SSM_CB)
    arow = a3.reshape(nblk, 3, 1, SSM_CB)
    avec = a3.reshape(nblk, 3, SUBLANES, LANES)

    def b_t(b):
        v = b.reshape(nl, ngb, SSM_CB, SSM_GROUP)
        return jnp.transpose(v, (0, 1, 3, 2)).reshape(nblk, SSM_GROUP, SSM_CB)

    def c_t(c):
        v = c.reshape(nl, ngb, SSM_GB, SSM_GROUP, SSM_STATE)
        return jnp.transpose(v, (0, 1, 4, 2, 3)).reshape(nblk, SSM_STATE, SSM_UB)

    bt = jnp.stack([b_t(b_re), b_t(b_im)], axis=1)
    ct = jnp.stack([c_t(c_re), c_t(c_im)], axis=1)
    blk = lambda i: (i, 0, 0, 0)
    return pl.pallas_call(
        _ssm_prep_kernel,
        grid=(nblk,),
        in_specs=[pl.BlockSpec((None, 3, 1, SSM_CB), blk),
                  pl.BlockSpec((None, 3, SUBLANES, LANES), blk),
                  pl.BlockSpec((None, 2, SSM_GROUP, SSM_CB), blk),
                  pl.BlockSpec((None, 2, SSM_STATE, SSM_UB), blk)],
        out_specs=[pl.BlockSpec((None, 2, SUBLANES, LANES), blk),
                   pl.BlockSpec((None, SSM_UB, 2 * SSM_CB), lambda i: (i, 0, 0)),
                   pl.BlockSpec((None, 2 * SSM_CB, SSM_UB), lambda i: (i, 0, 0))],
        out_shape=[jax.ShapeDtypeStruct((nblk, 2, SUBLANES, LANES), F32),
                   jax.ShapeDtypeStruct((nblk, SSM_UB, 2 * SSM_CB), BF16),
                   jax.ShapeDtypeStruct((nblk, 2 * SSM_CB, SSM_UB), BF16)],
        compiler_params=_params("parallel"),
        name="ssm_prep",
    )(arow, avec, bt, ct)


def _ssm_kernel(u_ref, bw_ref, cw_ref, abar_ref, d_ref, o_ref, s_ref, st_ref, *, tt, nb, ng):
    pitch = tt + SSM_PAD
    half = SSM_NJ // 2
    chains = [(b, gl) for b in range(nb) for gl in range(ng)]
    t_idx = pl.program_id(1)

    @pl.when(t_idx == 0)
    def _():
        st_ref[...] = jnp.zeros(st_ref.shape, F32)

    for q, (b, gl) in enumerate(chains):
        ucols = slice(gl * SSM_UB, (gl + 1) * SSM_UB)
        bu = jnp.dot(u_ref[b, :, ucols].astype(BF16), bw_ref[gl], preferred_element_type=F32)
        for j in range(SSM_NJ):
            s_ref[q, j * pitch:j * pitch + tt, :] = bu[:, j * LANES:(j + 1) * LANES]

    ar = [abar_ref[gl, 0] for gl in range(ng)]
    ai = [abar_ref[gl, 1] for gl in range(ng)]

    def step(t, carry):
        out = []
        for q, (b, gl) in enumerate(chains):
            xr, xi = carry[2 * q], carry[2 * q + 1]
            sq = s_ref.at[q]
            br = sq[pl.ds(t, half, stride=pitch), :]
            bi = sq[pl.ds(half * pitch + t, half, stride=pitch), :]
            nr = ar[gl] * xr - ai[gl] * xi + br
            ni = ar[gl] * xi + ai[gl] * xr + bi
            sq[pl.ds(t, half, stride=pitch), :] = nr
            sq[pl.ds(half * pitch + t, half, stride=pitch), :] = ni
            out += [nr, ni]
        return tuple(out)

    init = tuple(st_ref[i] for i in range(2 * len(chains)))
    fin = lax.fori_loop(0, tt, step, init, unroll=8)
    for i in range(2 * len(chains)):
        st_ref[i] = fin[i]

    for q, (b, gl) in enumerate(chains):
        ucols = slice(gl * SSM_UB, (gl + 1) * SSM_UB)
        acc = jnp.zeros((tt, SSM_UB), F32)
        for j in range(0, SSM_NJ, 2):
            xj = jnp.concatenate([s_ref[q, j * pitch:j * pitch + tt, :].astype(BF16),
                                  s_ref[q, (j + 1) * pitch:(j + 1) * pitch + tt, :].astype(BF16)], axis=1)
            part = jnp.dot(xj, cw_ref[gl, j * LANES:(j + 2) * LANES, :], preferred_element_type=F32)
            acc = acc + part if j < half else acc - part
        y = acc + d_ref[:, ucols] * u_ref[b, :, ucols]
        o_ref[b, :, ucols] = jax.nn.gelu(y).astype(o_ref.dtype)


def ssm_branch(proj3, bw, cw, abar, d_skip, li, tt, ng):
    bsz, seq, _ = proj3.shape
    ngb = SSM_WIDTH // SSM_UB
    uw = ng * SSM_UB
    col0 = 2 * CONV_WIDTH // uw
    blk0 = li * ngb // ng
    pitch = tt + SSM_PAD
    return pl.pallas_call(
        functools.partial(_ssm_kernel, tt=tt, nb=bsz, ng=ng),
        grid=(ngb // ng, seq // tt),
        in_specs=[pl.BlockSpec((bsz, tt, uw), lambda g, t: (0, t, col0 + g)),
                  pl.BlockSpec((ng, SSM_UB, 2 * SSM_CB), lambda g, t: (blk0 + g, 0, 0)),
                  pl.BlockSpec((ng, 2 * SSM_CB, SSM_UB), lambda g, t: (blk0 + g, 0, 0)),
                  pl.BlockSpec((ng, 2, SUBLANES, LANES), lambda g, t: (blk0 + g, 0, 0, 0)),
                  pl.BlockSpec((None, 1, uw), lambda g, t: (li, 0, g))],
        out_specs=pl.BlockSpec((bsz, tt, uw), lambda g, t: (0, t, g)),
        out_shape=jax.ShapeDtypeStruct((bsz, seq, SSM_WIDTH), BF16),
        scratch_shapes=[pltpu.VMEM((bsz * ng, SSM_NJ * pitch, LANES), F32),
                        pltpu.VMEM((2 * bsz * ng, SUBLANES, LANES), F32)],
        compiler_params=_params("parallel", "arbitrary"),
        name="ssm_branch",
    )(proj3, bw, cw, abar, d_skip)


def _zmix_kernel(hc_ref, ys_ref, wpw_ref, wga_ref, wgb_ref, la_ref, lb_ref, z_ref, pwb_ref, gab_ref, gbb_ref):
    @pl.when(pl.program_id(1) == 0)
    def _():
        pwb_ref[...] = wpw_ref[...].astype(BF16)
        gab_ref[...] = wga_ref[...].astype(BF16)
        gbb_ref[...] = wgb_ref[...].astype(BF16)

    hc = hc_ref[...]
    ys = ys_ref[...]
    for c in range(z_ref.shape[1] // MXU_COLS):
        sl = slice(c * MXU_COLS, (c + 1) * MXU_COLS)
        ya = jnp.dot(hc, pwb_ref[:, sl], preferred_element_type=F32)
        ga = jnp.dot(ys, gab_ref[:, sl], preferred_element_type=F32)
        gb = jnp.dot(ys, gbb_ref[:, sl], preferred_element_type=F32)
        yb = ga * _sigmoid(gb)
        z = _sigmoid(la_ref[:, sl]) * ya + _sigmoid(lb_ref[:, sl]) * yb
        z_ref[:, sl] = z.astype(z_ref.dtype)


def zmix(hc, ys, w_pw, w_glu, li, proj, tm, tn):
    m = hc.shape[0]
    nj = D_MODEL // tn
    gate0 = (2 * CONV_WIDTH + SSM_WIDTH) // tn
    return pl.pallas_call(
        _zmix_kernel,
        grid=(nj, m // tm),
        in_specs=[pl.BlockSpec((tm, CONV_WIDTH), lambda j, i: (i, 0)),
                  pl.BlockSpec((tm, SSM_WIDTH), lambda j, i: (i, 0)),
                  pl.BlockSpec((None, CONV_WIDTH, tn), lambda j, i: (li, 0, j)),
                  pl.BlockSpec((None, SSM_WIDTH, tn), lambda j, i: (li, 0, j)),
                  pl.BlockSpec((None, SSM_WIDTH, tn), lambda j, i: (li, 0, nj + j)),
                  pl.BlockSpec((tm, tn), lambda j, i: (i, gate0 + j)),
                  pl.BlockSpec((tm, tn), lambda j, i: (i, gate0 + nj + j))],
        out_specs=pl.BlockSpec((tm, tn), lambda j, i: (i, j)),
        out_shape=jax.ShapeDtypeStruct((m, D_MODEL), BF16),
        scratch_shapes=[pltpu.VMEM((CONV_WIDTH, tn), BF16),
                        pltpu.VMEM((SSM_WIDTH, tn), BF16),
                        pltpu.VMEM((SSM_WIDTH, tn), BF16)],
        compiler_params=_params("parallel", "arbitrary"),
        name="zmix",
    )(hc, ys, w_pw, w_glu, w_glu, proj, proj)


def _attn_kernel(q_ref, k_ref, v_ref, o_ref):
    scale = XA_HEAD_DIM ** -0.5
    for h in range(XA_HEADS):
        sl = slice(h * XA_HEAD_DIM, (h + 1) * XA_HEAD_DIM)
        s = lax.dot_general(q_ref[:, sl], k_ref[:, sl], (((1,), (1,)), ((), ())),
                            preferred_element_type=F32) * scale
        s = s - jnp.max(s, axis=-1, keepdims=True)
        p = jnp.exp(s)
        p = p / jnp.sum(p, axis=-1, keepdims=True)
        o = jnp.dot(p.astype(BF16), v_ref[:, sl], preferred_element_type=F32)
        o_ref[:, sl] = o.astype(o_ref.dtype)


def attention(q, kv, bsz, seq, tm):
    nt = seq // tm
    return pl.pallas_call(
        _attn_kernel,
        grid=(bsz, nt),
        in_specs=[pl.BlockSpec((tm, D_MODEL), lambda b, t: (b * nt + t, 0)),
                  pl.BlockSpec((MEM_LEN, D_MODEL), lambda b, t: (b, 0)),
                  pl.BlockSpec((MEM_LEN, D_MODEL), lambda b, t: (b, 1))],
        out_specs=pl.BlockSpec((tm, D_MODEL), lambda b, t: (b * nt + t, 0)),
        out_shape=jax.ShapeDtypeStruct((bsz * seq, D_MODEL), BF16),
        compiler_params=_params("parallel", "arbitrary"),
        name="attention",
    )(q, kv, kv)


FFN_HALO = 8
FFN_RB = 32


def _ffn_up_kernel(h_ref, wg_ref, wv_ref, dg_ref, dv_ref, o_ref, wgb_ref, wvb_ref, eg_ref, ev_ref,
                   *, tm, tn, nt):
    nslab = tn // LANES
    s = pl.program_id(1)

    @pl.when(s == 0)
    def _():
        wgb_ref[...] = wg_ref[...].astype(BF16)
        wvb_ref[...] = wv_ref[...].astype(BF16)

    @pl.when(s % nt == 0)
    def _():
        eg_ref[:, 0:FFN_HALO, :] = jnp.zeros((nslab, FFN_HALO, LANES), F32)
        ev_ref[:, 0:FFN_HALO, :] = jnp.zeros((nslab, FFN_HALO, LANES), F32)

    h = h_ref[...]
    for e_ref, wb_ref in ((eg_ref, wgb_ref), (ev_ref, wvb_ref)):
        for c2 in range(tn // MXU_COLS):
            u = jnp.dot(h, wb_ref[:, c2 * MXU_COLS:(c2 + 1) * MXU_COLS], preferred_element_type=F32)
            for half in range(MXU_COLS // LANES):
                c = c2 * (MXU_COLS // LANES) + half
                e_ref[c, FFN_HALO:FFN_HALO + tm, :] = u[:, half * LANES:(half + 1) * LANES]

    first = FFN_HALO - (FFN_K - 1)
    for c in range(nslab):
        sl = slice(c * LANES, (c + 1) * LANES)
        for rb in range(tm // FFN_RB):
            r0 = rb * FFN_RB
            g = dg_ref[0:1, sl] * eg_ref[c, first + r0:first + r0 + FFN_RB, :]
            v = dv_ref[0:1, sl] * ev_ref[c, first + r0:first + r0 + FFN_RB, :]
            for k in range(1, FFN_K):
                g = g + dg_ref[k:k + 1, sl] * eg_ref[c, first + r0 + k:first + r0 + k + FFN_RB, :]
                v = v + dv_ref[k:k + 1, sl] * ev_ref[c, first + r0 + k:first + r0 + k + FFN_RB, :]
            o_ref[r0:r0 + FFN_RB, sl] = (_silu(g) * v).astype(o_ref.dtype)

    eg_ref[:, 0:FFN_HALO, :] = eg_ref[:, tm:tm + FFN_HALO, :]
    ev_ref[:, 0:FFN_HALO, :] = ev_ref[:, tm:tm + FFN_HALO, :]


def ffn_up(h, w_up, dw_w, li, bsz, seq, tm, tn):
    nt = seq // tm
    nj = D_FF // tn
    slabs = (tn // LANES, FFN_HALO + tm, LANES)
    return pl.pallas_call(
        functools.partial(_ffn_up_kernel, tm=tm, tn=tn, nt=nt),
        grid=(nj, bsz * nt),
        in_specs=[pl.BlockSpec((tm, D_MODEL), lambda j, s: (s, 0)),
                  pl.BlockSpec((None, D_MODEL, tn), lambda j, s: (li, 0, j)),
                  pl.BlockSpec((None, D_MODEL, tn), lambda j, s: (li, 0, nj + j)),
                  pl.BlockSpec((None, FFN_K, tn), lambda j, s: (li, 0, j)),
                  pl.BlockSpec((None, FFN_K, tn), lambda j, s: (li, 0, nj + j))],
        out_specs=pl.BlockSpec((tm, tn), lambda j, s: (s, j)),
        out_shape=jax.ShapeDtypeStruct((bsz * seq, D_FF), BF16),
        scratch_shapes=[pltpu.VMEM((D_MODEL, tn), BF16),
                        pltpu.VMEM((D_MODEL, tn), BF16),
                        pltpu.VMEM(slabs, F32),
                        pltpu.VMEM(slabs, F32)],
        compiler_params=_params("parallel", "arbitrary"),
        name="ffn_up",
    )(h, w_up, w_up, dw_w, dw_w)


def kernel(x, mem, mix_norm_g, w_in, conv_dw_w, conv_dw_b, conv_ln_g, conv_ln_b, conv_w_pw,
           ssm_a_re, ssm_a_im, ssm_log_dt, ssm_b_re, ssm_b_im, ssm_c_re, ssm_c_im, ssm_d,
           ssm_w_glu, w_out, xa_norm_g, mem_norm_g, xa_w_q, xa_w_kv, xa_w_o,
           ffn_norm_g, ffn_w_up, ffn_dw_w, ffn_w_down, final_norm_g):
    bsz, seq, d = x.shape
    depth = w_in.shape[0]
    rows = bsz * seq
    xf = x.reshape(rows, d)
    memf = mem.reshape(bsz * MEM_LEN, d)

    w_down_b = ffn_w_down.astype(BF16)
    abar, bw, cw = ssm_prep(ssm_a_re, ssm_a_im, ssm_log_dt, ssm_b_re, ssm_b_im, ssm_c_re, ssm_c_im)
    d_skip = ssm_d.reshape(depth, 1, SSM_WIDTH)

    h = rmsnorm_rows(xf, mix_norm_g[0], 512, BF16)
    out = None
    for i in range(depth):
        proj = matmul_wcast(h, w_in, i, 1024, 1024, F32, "in_proj")
        hc = conv_branch(proj, conv_dw_w[i], conv_dw_b[i], conv_ln_g[i], conv_ln_b[i], bsz, seq, 512)
        ys = ssm_branch(proj.reshape(bsz, seq, -1), bw, cw, abar, d_skip, i, 512, 2).reshape(rows, SSM_WIDTH)
        z = zmix(hc, ys, conv_w_pw, ssm_w_glu, i, proj, 1024, 512)
        xf, h = matmul_res_norm(z, w_out, i, xf, xa_norm_g[i], 512, BF16, "out_proj")
        q = matmul_wcast(h, xa_w_q, i, 1024, 1024, BF16, "q_proj")
        mn = rmsnorm_rows(memf, mem_norm_g[i], 256, BF16)
        kv = matmul_wcast(mn, xa_w_kv, i, 512, 1024, BF16, "kv_proj")
        o = attention(q, kv, bsz, seq, 512)
        xf, h = matmul_res_norm(o, xa_w_o, i, xf, ffn_norm_g[i], 512, BF16, "o_proj")
        act = ffn_up(h, ffn_w_up, ffn_dw_w, i, bsz, seq, 1024, 512)
        if i + 1 < depth:
            xf, h = matmul_cols_res_norm(act, w_down_b, i, xf, mix_norm_g[i + 1],
                                         1024, 512, 2, BF16, True, "down_proj")
        else:
            out = matmul_cols_res_norm(act, w_down_b, i, xf, final_norm_g,
                                       1024, 512, 2, F32, False, "down_proj_final")[0]
    return out.reshape(bsz, seq, d)
```

```python
import functools
import math

import jax
import jax.numpy as jnp
from jax import lax
from jax.experimental import pallas as pl
from jax.experimental.pallas import tpu as pltpu

D_MODEL = 2048
MEM_LEN = 256
CONV_WIDTH = 1024
CONV_K = 31
SSM_WIDTH = 1024
SSM_GROUP = 16
SSM_GROUPS = SSM_WIDTH // SSM_GROUP
SSM_STATE = 64
XA_HEADS = 4
XA_HEAD_DIM = D_MODEL // XA_HEADS
D_FF = 5632
FFN_K = 3
EPS = 1e-6

LANES = 128
SUBLANES = 8
MXU_COLS = 256
VMEM_LIMIT = 56 * 1024 * 1024

BF16 = jnp.bfloat16
F32 = jnp.float32


def _params(*sem):
    return pltpu.CompilerParams(dimension_semantics=sem, vmem_limit_bytes=VMEM_LIMIT)


def _rms(x, g):
    return x * lax.rsqrt(jnp.mean(x * x, axis=-1, keepdims=True) + EPS) * g


def _sigmoid(x):
    return 0.5 * jnp.tanh(0.5 * x) + 0.5


def _silu(x):
    hx = 0.5 * x
    return hx * jnp.tanh(hx) + hx


def _rmsnorm_kernel(x_ref, g_ref, o_ref):
    o_ref[...] = _rms(x_ref[...], g_ref[...]).astype(o_ref.dtype)


def rmsnorm_rows(x, g, tm, out_dtype):
    m, d = x.shape
    return pl.pallas_call(
        _rmsnorm_kernel,
        grid=(m // tm,),
        in_specs=[pl.BlockSpec((tm, d), lambda i: (i, 0)),
                  pl.BlockSpec((1, d), lambda i: (0, 0))],
        out_specs=pl.BlockSpec((tm, d), lambda i: (i, 0)),
        out_shape=jax.ShapeDtypeStruct((m, d), out_dtype),
        compiler_params=_params("parallel"),
        name="rmsnorm",
    )(x, g.reshape(1, d))


def _mm_kernel(a_ref, w_ref, o_ref):
    o_ref[...] = jnp.dot(a_ref[...], w_ref[...], preferred_element_type=F32).astype(o_ref.dtype)


def matmul(a, w, li, tm, tn, out_dtype, name):
    m, k = a.shape
    n = w.shape[2]
    return pl.pallas_call(
        _mm_kernel,
        grid=(m // tm, n // tn),
        in_specs=[pl.BlockSpec((tm, k), lambda i, j: (i, 0)),
                  pl.BlockSpec((None, k, tn), lambda i, j: (li, 0, j))],
        out_specs=pl.BlockSpec((tm, tn), lambda i, j: (i, j)),
        out_shape=jax.ShapeDtypeStruct((m, n), out_dtype),
        compiler_params=_params("parallel", "arbitrary"),
        name=name,
    )(a, w)


def _mm_wcast_kernel(a_ref, w_ref, o_ref, wb_ref):
    @pl.when(pl.program_id(1) == 0)
    def _():
        wb_ref[...] = w_ref[...].astype(BF16)

    o_ref[...] = jnp.dot(a_ref[...], wb_ref[...], preferred_element_type=F32).astype(o_ref.dtype)


def matmul_wcast(a, w, li, tm, tn, out_dtype, name):
    m, k = a.shape
    n = w.shape[2]
    return pl.pallas_call(
        _mm_wcast_kernel,
        grid=(n // tn, m // tm),
        in_specs=[pl.BlockSpec((tm, k), lambda j, i: (i, 0)),
                  pl.BlockSpec((None, k, tn), lambda j, i: (li, 0, j))],
        out_specs=pl.BlockSpec((tm, tn), lambda j, i: (i, j)),
        out_shape=jax.ShapeDtypeStruct((m, n), out_dtype),
        scratch_shapes=[pltpu.VMEM((k, tn), BF16)],
        compiler_params=_params("parallel", "arbitrary"),
        name=name,
    )(a, w)


def _mm_res_norm_kernel(a_ref, w_ref, x_ref, g_ref, xo_ref, h_ref, wb_ref):
    @pl.when(pl.program_id(0) == 0)
    def _():
        wb_ref[...] = w_ref[...].astype(BF16)

    xn = x_ref[...] + jnp.dot(a_ref[...], wb_ref[...], preferred_element_type=F32)
    xo_ref[...] = xn
    h_ref[...] = _rms(xn, g_ref[...]).astype(h_ref.dtype)


def matmul_res_norm(a, w, li, x, g, tm, h_dtype, name):
    m, kdim = a.shape
    n = w.shape[2]
    row = lambda i: (i, 0)
    return pl.pallas_call(
        _mm_res_norm_kernel,
        grid=(m // tm,),
        in_specs=[pl.BlockSpec((tm, kdim), row),
                  pl.BlockSpec((None, kdim, n), lambda i: (li, 0, 0), pipeline_mode=pl.Buffered(1)),
                  pl.BlockSpec((tm, n), row),
                  pl.BlockSpec((1, n), lambda i: (0, 0))],
        out_specs=[pl.BlockSpec((tm, n), row), pl.BlockSpec((tm, n), row)],
        out_shape=[jax.ShapeDtypeStruct((m, n), F32), jax.ShapeDtypeStruct((m, n), h_dtype)],
        scratch_shapes=[pltpu.VMEM((kdim, n), BF16)],
        compiler_params=_params("arbitrary"),
        name=name,
    )(a, w, x, g.reshape(1, n))


def _mm_cols_res_norm_kernel(a_ref, w_ref, x_ref, g_ref, *refs, nk, nj, tn, emit_x):
    if emit_x:
        xo_ref, h_ref, row_ref = refs
    else:
        h_ref, row_ref = refs
    k = pl.program_id(1)
    j = pl.program_id(2)

    @pl.when(k == 0)
    def _():
        row_ref[j] = x_ref[...]

    row_ref[j] += jnp.dot(a_ref[...], w_ref[...], preferred_element_type=F32)

    if emit_x:
        @pl.when(k == nk - 1)
        def _():
            xo_ref[...] = row_ref[j]

    @pl.when((k == nk - 1) & (j == nj - 1))
    def _():
        ssq = jnp.zeros((row_ref.shape[1], 1), F32)
        for jj in range(nj):
            blk = row_ref[jj]
            ssq = ssq + jnp.sum(blk * blk, axis=-1, keepdims=True)
        inv = lax.rsqrt(ssq * (1.0 / (nj * tn)) + EPS)
        for jj in range(nj):
            sl = slice(jj * tn, (jj + 1) * tn)
            h_ref[:, sl] = (row_ref[jj] * inv * g_ref[:, sl]).astype(h_ref.dtype)


def matmul_cols_res_norm(a, w, li, x, g, tm, tn, nk, h_dtype, emit_x, name):
    m, kdim = a.shape
    n = w.shape[2]
    nj = n // tn
    tk = kdim // nk
    last = nk - 1
    out_specs = [pl.BlockSpec((tm, n), lambda i, k, j: (i, 0))]
    out_shape = [jax.ShapeDtypeStruct((m, n), h_dtype)]
    if emit_x:
        out_specs = [pl.BlockSpec((tm, tn), lambda i, k, j: (i, jnp.where(k == last, j, 0)))] + out_specs
        out_shape = [jax.ShapeDtypeStruct((m, n), F32)] + out_shape
    return pl.pallas_call(
        functools.partial(_mm_cols_res_norm_kernel, nk=nk, nj=nj, tn=tn, emit_x=emit_x),
        grid=(m // tm, nk, nj),
        in_specs=[pl.BlockSpec((tm, tk), lambda i, k, j: (i, k)),
                  pl.BlockSpec((None, tk, tn), lambda i, k, j: (li, k, j)),
                  pl.BlockSpec((tm, tn), lambda i, k, j: (i, jnp.where(k == 0, j, nj - 1))),
                  pl.BlockSpec((1, n), lambda i, k, j: (0, 0))],
        out_specs=out_specs,
        out_shape=out_shape,
        scratch_shapes=[pltpu.VMEM((nj, tm, tn), F32)],
        compiler_params=_params("parallel", "arbitrary", "arbitrary"),
        name=name,
    )(a, w, x, g.reshape(1, n))


CONV_HALO = 32
CONV_RB = 32


def _conv_kernel(a_ref, b_ref, w_ref, bias_ref, lng_ref, lnb_ref, o_ref, ext_ref, y_ref, *, tm):
    nslab = CONV_WIDTH // LANES
    t = pl.program_id(1)

    @pl.when(t == 0)
    def _():
        ext_ref[:, 0:CONV_HALO, :] = jnp.zeros((nslab, CONV_HALO, LANES), F32)

    for c in range(nslab):
        sl = slice(c * LANES, (c + 1) * LANES)
        ext_ref[c, CONV_HALO:CONV_HALO + tm, :] = a_ref[:, sl] * _sigmoid(b_ref[:, sl])

    first = CONV_HALO - (CONV_K - 1)
    for c in range(nslab):
        sl = slice(c * LANES, (c + 1) * LANES)

        def row_block(rb, carry, c=c, sl=sl):
            r0 = pl.multiple_of(rb * CONV_RB, CONV_RB)
            acc = [jnp.broadcast_to(bias_ref[:, sl], (CONV_RB, LANES)), None]
            for k in range(CONV_K):
                term = w_ref[k:k + 1, sl] * ext_ref[c, pl.ds(r0 + first + k, CONV_RB), :]
                acc[k % 2] = term if acc[k % 2] is None else acc[k % 2] + term
            y_ref[pl.ds(r0, CONV_RB), sl] = acc[0] + acc[1]
            return carry

        lax.fori_loop(0, tm // CONV_RB, row_block, 0, unroll=2)

    ext_ref[:, 0:CONV_HALO, :] = ext_ref[:, tm:tm + CONV_HALO, :]

    def ln_block(rb, carry):
        r0 = pl.multiple_of(rb * CONV_RB, CONV_RB)
        y = y_ref[pl.ds(r0, CONV_RB), :]
        mu = jnp.mean(y, axis=-1, keepdims=True)
        yc = y - mu
        yn = yc * lax.rsqrt(jnp.mean(yc * yc, axis=-1, keepdims=True) + EPS)
        yn = yn * lng_ref[...] + lnb_ref[...]
        o_ref[pl.ds(r0, CONV_RB), :] = _silu(yn).astype(o_ref.dtype)
        return carry

    lax.fori_loop(0, tm // CONV_RB, ln_block, 0, unroll=4)


def conv_branch(proj, dw_w, dw_b, ln_g, ln_b, bsz, seq, tm):
    nt = seq // tm
    cw = CONV_WIDTH
    row = lambda b, t: (b * nt + t, 0)
    vec = lambda b, t: (0, 0)
    return pl.pallas_call(
        functools.partial(_conv_kernel, tm=tm),
        grid=(bsz, nt),
        in_specs=[pl.BlockSpec((tm, cw), lambda b, t: (b * nt + t, 0)),
                  pl.BlockSpec((tm, cw), lambda b, t: (b * nt + t, 1)),
                  pl.BlockSpec((CONV_K, cw), vec),
                  pl.BlockSpec((1, cw), vec),
                  pl.BlockSpec((1, cw), vec),
                  pl.BlockSpec((1, cw), vec)],
        out_specs=pl.BlockSpec((tm, cw), row),
        out_shape=jax.ShapeDtypeStruct((bsz * seq, cw), BF16),
        scratch_shapes=[pltpu.VMEM((cw // LANES, CONV_HALO + tm, LANES), F32),
                        pltpu.VMEM((tm, cw), F32)],
        compiler_params=_params("parallel", "arbitrary"),
        name="conv_branch",
    )(proj, proj, dw_w, dw_b.reshape(1, cw), ln_g.reshape(1, cw), ln_b.reshape(1, cw))


SSM_GB = 16
SSM_UB = SSM_GB * SSM_GROUP
SSM_CB = SSM_GB * SSM_STATE
SSM_NJ = 2 * SSM_CB // LANES
SSM_PAD = 4


def _zoh(ar, ai, ldt):
    ar = jnp.minimum(ar, -1e-4)
    dt = jnp.exp(ldt)
    mag = jnp.exp(dt * ar)
    return mag * jnp.cos(dt * ai), mag * jnp.sin(dt * ai), ar


def _ssm_prep_kernel(arow_ref, avec_ref, bt_ref, ct_ref, abar_ref, bw_ref, cw_ref):
    are, aim, _ = _zoh(avec_ref[0], avec_ref[1], avec_ref[2])
    abar_ref[0] = are
    abar_ref[1] = aim

    abar_re, abar_im, ar = _zoh(arow_ref[0], arow_ref[1], arow_ref[2])
    ai = arow_ref[1]
    den = ar * ar + ai * ai
    nr = abar_re - 1.0
    z_re = (nr * ar + abar_im * ai) / den
    z_im = (abar_im * ar - nr * ai) / den
    br = bt_ref[0]
    bi = bt_ref[1]
    bbar = (z_re * br - z_im * bi, z_re * bi + z_im * br)

    rg = lax.broadcasted_iota(jnp.int32, (SSM_UB, SSM_CB), 0) // SSM_GROUP
    cg = lax.broadcasted_iota(jnp.int32, (SSM_UB, SSM_CB), 1) // SSM_STATE
    for part in range(2):
        tiled = jnp.concatenate([bbar[part]] * SSM_GB, axis=0)
        bw_ref[:, part * SSM_CB:(part + 1) * SSM_CB] = jnp.where(rg == cg, tiled, 0.0).astype(BF16)

    rg = lax.broadcasted_iota(jnp.int32, (SSM_CB, SSM_UB), 0) // SSM_STATE
    cg = lax.broadcasted_iota(jnp.int32, (SSM_CB, SSM_UB), 1) // SSM_GROUP
    for part in range(2):
        tiled = jnp.concatenate([ct_ref[part]] * SSM_GB, axis=0)
        cw_ref[part * SSM_CB:(part + 1) * SSM_CB, :] = jnp.where(rg == cg, tiled, 0.0).astype(BF16)


def ssm_prep(a_re, a_im, log_dt, b_re, b_im, c_re, c_im):
    nl = a_re.shape[0]
    ngb = SSM_GROUPS // SSM_GB
    nblk = nl * ngb
    ldt = jnp.broadcast_to(log_dt[:, :, None], a_re.shape)
    a3 = jnp.stack([a_re, a_im, ldt], axis=1).reshape(nl, 3, ngb, SSM_CB)
    a3 = jnp.transpose(a3, (0, 2, 1, 3)).reshape(nblk, 3, SSM_CB)
    arow = a3.reshape(nblk, 3, 1, SSM_CB)
    avec = a3.reshape(nblk, 3, SUBLANES, LANES)

    def b_t(b):
        v = b.reshape(nl, ngb, SSM_CB, SSM_GROUP)
        return jnp.transpose(v, (0, 1, 3, 2)).reshape(nblk, SSM_GROUP, SSM_CB)

    def c_t(c):
        v = c.reshape(nl, ngb, SSM_GB, SSM_GROUP, SSM_STATE)
        return jnp.transpose(v, (0, 1, 4, 2, 3)).reshape(nblk, SSM_STATE, SSM_UB)

    bt = jnp.stack([b_t(b_re), b_t(b_im)], axis=1)
    ct = jnp.stack([c_t(c_re), c_t(c_im)], axis=1)
    blk = lambda i: (i, 0, 0, 0)
    return pl.pallas_call(
        _ssm_prep_kernel,
        grid=(nblk,),
        in_specs=[pl.BlockSpec((None, 3, 1, SSM_CB), blk),
                  pl.BlockSpec((None, 3, SUBLANES, LANES), blk),
                  pl.BlockSpec((None, 2, SSM_GROUP, SSM_CB), blk),
                  pl.BlockSpec((None, 2, SSM_STATE, SSM_UB), blk)],
        out_specs=[pl.BlockSpec((None, 2, SUBLANES, LANES), blk),
                   pl.BlockSpec((None, SSM_UB, 2 * SSM_CB), lambda i: (i, 0, 0)),
                   pl.BlockSpec((None, 2 * SSM_CB, SSM_UB), lambda i: (i, 0, 0))],
        out_shape=[jax.ShapeDtypeStruct((nblk, 2, SUBLANES, LANES), F32),
                   jax.ShapeDtypeStruct((nblk, SSM_UB, 2 * SSM_CB), BF16),
                   jax.ShapeDtypeStruct((nblk, 2 * SSM_CB, SSM_UB), BF16)],
        compiler_params=_params("parallel"),
        name="ssm_prep",
    )(arow, avec, bt, ct)


def _ssm_kernel(u_ref, bw_ref, cw_ref, abar_ref, d_ref, o_ref, s_ref, st_ref, *, tt, nb, ng):
    pitch = tt + SSM_PAD
    half = SSM_NJ // 2
    chains = [(b, gl) for b in range(nb) for gl in range(ng)]
    t_idx = pl.program_id(1)

    @pl.when(t_idx == 0)
    def _():
        st_ref[...] = jnp.zeros(st_ref.shape, F32)

    for q, (b, gl) in enumerate(chains):
        ucols = slice(gl * SSM_UB, (gl + 1) * SSM_UB)
        bu = jnp.dot(u_ref[b, :, ucols].astype(BF16), bw_ref[gl], preferred_element_type=F32)
        for j in range(SSM_NJ):
            s_ref[q, j * pitch:j * pitch + tt, :] = bu[:, j * LANES:(j + 1) * LANES]

    ar = [abar_ref[gl, 0] for gl in range(ng)]
    ai = [abar_ref[gl, 1] for gl in range(ng)]

    def step(t, carry):
        out = []
        for q, (b, gl) in enumerate(chains):
            xr, xi = carry[2 * q], carry[2 * q + 1]
            sq = s_ref.at[q]
            br = sq[pl.ds(t, half, stride=pitch), :]
            bi = sq[pl.ds(half * pitch + t, half, stride=pitch), :]
            nr = ar[gl] * xr - ai[gl] * xi + br
            ni = ar[gl] * xi + ai[gl] * xr + bi
            sq[pl.ds(t, half, stride=pitch), :] = nr
            sq[pl.ds(half * pitch + t, half, stride=pitch), :] = ni
            out += [nr, ni]
        return tuple(out)

    init = tuple(st_ref[i] for i in range(2 * len(chains)))
    fin = lax.fori_loop(0, tt, step, init, unroll=8)
    for i in range(2 * len(chains)):
        st_ref[i] = fin[i]

    for q, (b, gl) in enumerate(chains):
        ucols = slice(gl * SSM_UB, (gl + 1) * SSM_UB)
        acc = jnp.zeros((tt, SSM_UB), F32)
        for j in range(0, SSM_NJ, 2):
            xj = jnp.concatenate([s_ref[q, j * pitch:j * pitch + tt, :].astype(BF16),
                                  s_ref[q, (j + 1) * pitch:(j + 1) * pitch + tt, :].astype(BF16)], axis=1)
            part = jnp.dot(xj, cw_ref[gl, j * LANES:(j + 2) * LANES, :], preferred_element_type=F32)
            acc = acc + part if j < half else acc - part
        y = acc + d_ref[:, ucols] * u_ref[b, :, ucols]
        o_ref[b, :, ucols] = jax.nn.gelu(y).astype(o_ref.dtype)


def ssm_branch(proj3, bw, cw, abar, d_skip, li, tt, ng):
    bsz, seq, _ = proj3.shape
    ngb = SSM_WIDTH // SSM_UB
    uw = ng * SSM_UB
    col0 = 2 * CONV_WIDTH // uw
    blk0 = li * ngb // ng
    pitch = tt + SSM_PAD
    return pl.pallas_call(
        functools.partial(_ssm_kernel, tt=tt, nb=bsz, ng=ng),
        grid=(ngb // ng, seq // tt),
        in_specs=[pl.BlockSpec((bsz, tt, uw), lambda g, t: (0, t, col0 + g)),
                  pl.BlockSpec((ng, SSM_UB, 2 * SSM_CB), lambda g, t: (blk0 + g, 0, 0)),
                  pl.BlockSpec((ng, 2 * SSM_CB, SSM_UB), lambda g, t: (blk0 + g, 0, 0)),
                  pl.BlockSpec((ng, 2, SUBLANES, LANES), lambda g, t: (blk0 + g, 0, 0, 0)),
                  pl.BlockSpec((None, 1, uw), lambda g, t: (li, 0, g))],
        out_specs=pl.BlockSpec((bsz, tt, uw), lambda g, t: (0, t, g)),
        out_shape=jax.ShapeDtypeStruct((bsz, seq, SSM_WIDTH), BF16),
        scratch_shapes=[pltpu.VMEM((bsz * ng, SSM_NJ * pitch, LANES), F32),
                        pltpu.VMEM((2 * bsz * ng, SUBLANES, LANES), F32)],
        compiler_params=_params("parallel", "arbitrary"),
        name="ssm_branch",
    )(proj3, bw, cw, abar, d_skip)


def _zmix_kernel(hc_ref, ys_ref, wpw_ref, wga_ref, wgb_ref, la_ref, lb_ref, z_ref, pwb_ref, gab_ref, gbb_ref):
    @pl.when(pl.program_id(1) == 0)
    def _():
        pwb_ref[...] = wpw_ref[...].astype(BF16)
        gab_ref[...] = wga_ref[...].astype(BF16)
        gbb_ref[...] = wgb_ref[...].astype(BF16)

    hc = hc_ref[...]
    ys = ys_ref[...]
    for c in range(z_ref.shape[1] // MXU_COLS):
        sl = slice(c * MXU_COLS, (c + 1) * MXU_COLS)
        ya = jnp.dot(hc, pwb_ref[:, sl], preferred_element_type=F32)
        ga = jnp.dot(ys, gab_ref[:, sl], preferred_element_type=F32)
        gb = jnp.dot(ys, gbb_ref[:, sl], preferred_element_type=F32)
        yb = ga * _sigmoid(gb)
        z = _sigmoid(la_ref[:, sl]) * ya + _sigmoid(lb_ref[:, sl]) * yb
        z_ref[:, sl] = z.astype(z_ref.dtype)


def zmix(hc, ys, w_pw, w_glu, li, proj, tm, tn):
    m = hc.shape[0]
    nj = D_MODEL // tn
    gate0 = (2 * CONV_WIDTH + SSM_WIDTH) // tn
    return pl.pallas_call(
        _zmix_kernel,
        grid=(nj, m // tm),
        in_specs=[pl.BlockSpec((tm, CONV_WIDTH), lambda j, i: (i, 0)),
                  pl.BlockSpec((tm, SSM_WIDTH), lambda j, i: (i, 0)),
                  pl.BlockSpec((None, CONV_WIDTH, tn), lambda j, i: (li, 0, j)),
                  pl.BlockSpec((None, SSM_WIDTH, tn), lambda j, i: (li, 0, j)),
                  pl.BlockSpec((None, SSM_WIDTH, tn), lambda j, i: (li, 0, nj + j)),
                  pl.BlockSpec((tm, tn), lambda j, i: (i, gate0 + j)),
                  pl.BlockSpec((tm, tn), lambda j, i: (i, gate0 + nj + j))],
        out_specs=pl.BlockSpec((tm, tn), lambda j, i: (i, j)),
        out_shape=jax.ShapeDtypeStruct((m, D_MODEL), BF16),
        scratch_shapes=[pltpu.VMEM((CONV_WIDTH, tn), BF16),
                        pltpu.VMEM((SSM_WIDTH, tn), BF16),
                        pltpu.VMEM((SSM_WIDTH, tn), BF16)],
        compiler_params=_params("parallel", "arbitrary"),
        name="zmix",
    )(hc, ys, w_pw, w_glu, w_glu, proj, proj)


def _attn_kernel(q_ref, k_ref, v_ref, o_ref):
    scale = XA_HEAD_DIM ** -0.5
    for h in range(XA_HEADS):
        sl = slice(h * XA_HEAD_DIM, (h + 1) * XA_HEAD_DIM)
        s = lax.dot_general(q_ref[:, sl], k_ref[:, sl], (((1,), (1,)), ((), ())),
                            preferred_element_type=F32) * scale
        s = s - jnp.max(s, axis=-1, keepdims=True)
        p = jnp.exp(s)
        p = p / jnp.sum(p, axis=-1, keepdims=True)
        o = jnp.dot(p.astype(BF16), v_ref[:, sl], preferred_element_type=F32)
        o_ref[:, sl] = o.astype(o_ref.dtype)


def attention(q, kv, bsz, seq, tm):
    nt = seq // tm
    return pl.pallas_call(
        _attn_kernel,
        grid=(bsz, nt),
        in_specs=[pl.BlockSpec((tm, D_MODEL), lambda b, t: (b * nt + t, 0)),
                  pl.BlockSpec((MEM_LEN, D_MODEL), lambda b, t: (b, 0)),
                  pl.BlockSpec((MEM_LEN, D_MODEL), lambda b, t: (b, 1))],
        out_specs=pl.BlockSpec((tm, D_MODEL), lambda b, t: (b * nt + t, 0)),
        out_shape=jax.ShapeDtypeStruct((bsz * seq, D_MODEL), BF16),
        compiler_params=_params("parallel", "arbitrary"),
        name="attention",
    )(q, kv, kv)


FFN_HALO = 8
FFN_RB = 32


def _ffn_up_kernel(h_ref, wg_ref, wv_ref, dg_ref, dv_ref, o_ref, wgb_ref, wvb_ref,
                   dgh_ref, eg_ref, ev_ref, *, tm, tn, nt):
    nslab = tn // LANES
    s = pl.program_id(1)

    @pl.when(s == 0)
    def _():
        wgb_ref[...] = wg_ref[...].astype(BF16)
        wvb_ref[...] = wv_ref[...].astype(BF16)
        dgh_ref[...] = 0.5 * dg_ref[...]

    @pl.when(s % nt == 0)
    def _():
        eg_ref[:, 0:FFN_HALO, :] = jnp.zeros((nslab, FFN_HALO, LANES), F32)
        ev_ref[:, 0:FFN_HALO, :] = jnp.zeros((nslab, FFN_HALO, LANES), F32)

    h = h_ref[...]
    for e_ref, wb_ref in ((eg_ref, wgb_ref), (ev_ref, wvb_ref)):
        for c2 in range(tn // MXU_COLS):
            u = jnp.dot(h, wb_ref[:, c2 * MXU_COLS:(c2 + 1) * MXU_COLS], preferred_element_type=F32)
            for half in range(MXU_COLS // LANES):
                c = c2 * (MXU_COLS // LANES) + half
                e_ref[c, FFN_HALO:FFN_HALO + tm, :] = u[:, half * LANES:(half + 1) * LANES]

    first = FFN_HALO - (FFN_K - 1)
    for c in range(nslab):
        sl = slice(c * LANES, (c + 1) * LANES)
        for rb in range(tm // FFN_RB):
            r0 = rb * FFN_RB
            hg = dgh_ref[0:1, sl] * eg_ref[c, first + r0:first + r0 + FFN_RB, :]
            v = dv_ref[0:1, sl] * ev_ref[c, first + r0:first + r0 + FFN_RB, :]
            for k in range(1, FFN_K):
                hg = hg + dgh_ref[k:k + 1, sl] * eg_ref[c, first + r0 + k:first + r0 + k + FFN_RB, :]
                v = v + dv_ref[k:k + 1, sl] * ev_ref[c, first + r0 + k:first + r0 + k + FFN_RB, :]
            o_ref[r0:r0 + FFN_RB, sl] = ((hg * jnp.tanh(hg) + hg) * v).astype(o_ref.dtype)

    eg_ref[:, 0:FFN_HALO, :] = eg_ref[:, tm:tm + FFN_HALO, :]
    ev_ref[:, 0:FFN_HALO, :] = ev_ref[:, tm:tm + FFN_HALO, :]


def ffn_up(h, w_up, dw_w, li, bsz, seq, tm, tn):
    nt = seq // tm
    nj = D_FF // tn
    slabs = (tn // LANES, FFN_HALO + tm, LANES)
    return pl.pallas_call(
        functools.partial(_ffn_up_kernel, tm=tm, tn=tn, nt=nt),
        grid=(nj, bsz * nt),
        in_specs=[pl.BlockSpec((tm, D_MODEL), lambda j, s: (s, 0)),
                  pl.BlockSpec((None, D_MODEL, tn), lambda j, s: (li, 0, j)),
                  pl.BlockSpec((None, D_MODEL, tn), lambda j, s: (li, 0, nj + j)),
                  pl.BlockSpec((None, FFN_K, tn), lambda j, s: (li, 0, j)),
                  pl.BlockSpec((None, FFN_K, tn), lambda j, s: (li, 0, nj + j))],
        out_specs=pl.BlockSpec((tm, tn), lambda j, s: (s, j)),
        out_shape=jax.ShapeDtypeStruct((bsz * seq, D_FF), BF16),
        scratch_shapes=[pltpu.VMEM((D_MODEL, tn), BF16),
                        pltpu.VMEM((D_MODEL, tn), BF16),
                        pltpu.VMEM((FFN_K, tn), F32),
                        pltpu.VMEM(slabs, F32),
                        pltpu.VMEM(slabs, F32)],
        compiler_params=_params("parallel", "arbitrary"),
        name="ffn_up",
    )(h, w_up, w_up, dw_w, dw_w)


def kernel(x, mem, mix_norm_g, w_in, conv_dw_w, conv_dw_b, conv_ln_g, conv_ln_b, conv_w_pw,
           ssm_a_re, ssm_a_im, ssm_log_dt, ssm_b_re, ssm_b_im, ssm_c_re, ssm_c_im, ssm_d,
           ssm_w_glu, w_out, xa_norm_g, mem_norm_g, xa_w_q, xa_w_kv, xa_w_o,
           ffn_norm_g, ffn_w_up, ffn_dw_w, ffn_w_down, final_norm_g):
    bsz, seq, d = x.shape
    depth = w_in.shape[0]
    rows = bsz * seq
    xf = x.reshape(rows, d)
    memf = mem.reshape(bsz * MEM_LEN, d)

    w_down_b = ffn_w_down.astype(BF16)
    abar, bw, cw = ssm_prep(ssm_a_re, ssm_a_im, ssm_log_dt, ssm_b_re, ssm_b_im, ssm_c_re, ssm_c_im)
    d_skip = ssm_d.reshape(depth, 1, SSM_WIDTH)

    h = rmsnorm_rows(xf, mix_norm_g[0], 512, BF16)
    out = None
    for i in range(depth):
        proj = matmul_wcast(h, w_in, i, 1024, 1024, F32, "in_proj")
        hc = conv_branch(proj, conv_dw_w[i], conv_dw_b[i], conv_ln_g[i], conv_ln_b[i], bsz, seq, 512)
        ys = ssm_branch(proj.reshape(bsz, seq, -1), bw, cw, abar, d_skip, i, 512, 2).reshape(rows, SSM_WIDTH)
        z = zmix(hc, ys, conv_w_pw, ssm_w_glu, i, proj, 1024, 512)
        xf, h = matmul_res_norm(z, w_out, i, xf, xa_norm_g[i], 512, BF16, "out_proj")
        q = matmul_wcast(h, xa_w_q, i, 1024, 1024, BF16, "q_proj")
        mn = rmsnorm_rows(memf, mem_norm_g[i], 256, BF16)
        kv = matmul_wcast(mn, xa_w_kv, i, 512, 1024, BF16, "kv_proj")
        o = attention(q, kv, bsz, seq, 1024)
        xf, h = matmul_res_norm(o, xa_w_o, i, xf, ffn_norm_g[i], 512, BF16, "o_proj")
        act = ffn_up(h, ffn_w_up, ffn_dw_w, i, bsz, seq, 1024, 512)
        if i + 1 < depth:
            xf, h = matmul_cols_res_norm(act, w_down_b, i, xf, mix_norm_g[i + 1],
                                         1024, 512, 2, BF16, True, "down_proj")
        else:
            out = matmul_cols_res_norm(act, w_down_b, i, xf, final_norm_g,
                                       1024, 512, 2, F32, False, "down_proj_final")[0]
    return out.reshape(bsz, seq, d)
```

```python
import functools
import math

import jax
import jax.numpy as jnp
from jax import lax
from jax.experimental import pallas as pl
from jax.experimental.pallas import tpu as pltpu

D_MODEL = 2048
MEM_LEN = 256
CONV_WIDTH = 1024
CONV_K = 31
SSM_WIDTH = 1024
SSM_GROUP = 16
SSM_GROUPS = SSM_WIDTH // SSM_GROUP
SSM_STATE = 64
XA_HEADS = 4
XA_HEAD_DIM = D_MODEL // XA_HEADS
D_FF = 5632
FFN_K = 3
EPS = 1e-6

LANES = 128
SUBLANES = 8
MXU_COLS = 256
VMEM_LIMIT = 56 * 1024 * 1024

BF16 = jnp.bfloat16
F32 = jnp.float32


def _params(*sem):
    return pltpu.CompilerParams(dimension_semantics=sem, vmem_limit_bytes=VMEM_LIMIT)


def _rms(x, g):
    return x * lax.rsqrt(jnp.mean(x * x, axis=-1, keepdims=True) + EPS) * g


def _sigmoid(x):
    return 0.5 * jnp.tanh(0.5 * x) + 0.5


def _silu(x):
    hx = 0.5 * x
    return hx * jnp.tanh(hx) + hx


def _rmsnorm_kernel(x_ref, g_ref, o_ref):
    o_ref[...] = _rms(x_ref[...], g_ref[...]).astype(o_ref.dtype)


def rmsnorm_rows(x, g, tm, out_dtype):
    m, d = x.shape
    return pl.pallas_call(
        _rmsnorm_kernel,
        grid=(m // tm,),
        in_specs=[pl.BlockSpec((tm, d), lambda i: (i, 0)),
                  pl.BlockSpec((1, d), lambda i: (0, 0))],
        out_specs=pl.BlockSpec((tm, d), lambda i: (i, 0)),
        out_shape=jax.ShapeDtypeStruct((m, d), out_dtype),
        compiler_params=_params("parallel"),
        name="rmsnorm",
    )(x, g.reshape(1, d))


def _mm_kernel(a_ref, w_ref, o_ref):
    o_ref[...] = jnp.dot(a_ref[...], w_ref[...], preferred_element_type=F32).astype(o_ref.dtype)


def matmul(a, w, li, tm, tn, out_dtype, name):
    m, k = a.shape
    n = w.shape[2]
    return pl.pallas_call(
        _mm_kernel,
        grid=(m // tm, n // tn),
        in_specs=[pl.BlockSpec((tm, k), lambda i, j: (i, 0)),
                  pl.BlockSpec((None, k, tn), lambda i, j: (li, 0, j))],
        out_specs=pl.BlockSpec((tm, tn), lambda i, j: (i, j)),
        out_shape=jax.ShapeDtypeStruct((m, n), out_dtype),
        compiler_params=_params("parallel", "arbitrary"),
        name=name,
    )(a, w)


def _mm_wcast_kernel(a_ref, w_ref, o_ref, wb_ref):
    @pl.when(pl.program_id(1) == 0)
    def _():
        wb_ref[...] = w_ref[...].astype(BF16)

    o_ref[...] = jnp.dot(a_ref[...], wb_ref[...], preferred_element_type=F32).astype(o_ref.dtype)


def matmul_wcast(a, w, li, tm, tn, out_dtype, name):
    m, k = a.shape
    n = w.shape[2]
    return pl.pallas_call(
        _mm_wcast_kernel,
        grid=(n // tn, m // tm),
        in_specs=[pl.BlockSpec((tm, k), lambda j, i: (i, 0)),
                  pl.BlockSpec((None, k, tn), lambda j, i: (li, 0, j))],
        out_specs=pl.BlockSpec((tm, tn), lambda j, i: (i, j)),
        out_shape=jax.ShapeDtypeStruct((m, n), out_dtype),
        scratch_shapes=[pltpu.VMEM((k, tn), BF16)],
        compiler_params=_params("parallel", "arbitrary"),
        name=name,
    )(a, w)


def _mm_res_norm_kernel(a_ref, w_ref, x_ref, g_ref, xo_ref, h_ref, wb_ref):
    @pl.when(pl.program_id(0) == 0)
    def _():
        wb_ref[...] = w_ref[...].astype(BF16)

    xn = x_ref[...] + jnp.dot(a_ref[...], wb_ref[...], preferred_element_type=F32)
    xo_ref[...] = xn
    h_ref[...] = _rms(xn, g_ref[...]).astype(h_ref.dtype)


def matmul_res_norm(a, w, li, x, g, tm, h_dtype, name):
    m, kdim = a.shape
    n = w.shape[2]
    row = lambda i: (i, 0)
    return pl.pallas_call(
        _mm_res_norm_kernel,
        grid=(m // tm,),
        in_specs=[pl.BlockSpec((tm, kdim), row),
                  pl.BlockSpec((None, kdim, n), lambda i: (li, 0, 0), pipeline_mode=pl.Buffered(1)),
                  pl.BlockSpec((tm, n), row),
                  pl.BlockSpec((1, n), lambda i: (0, 0))],
        out_specs=[pl.BlockSpec((tm, n), row), pl.BlockSpec((tm, n), row)],
        out_shape=[jax.ShapeDtypeStruct((m, n), F32), jax.ShapeDtypeStruct((m, n), h_dtype)],
        scratch_shapes=[pltpu.VMEM((kdim, n), BF16)],
        compiler_params=_params("arbitrary"),
        name=name,
    )(a, w, x, g.reshape(1, n))


def _mm_cols_res_norm_kernel(a_ref, w_ref, x_ref, g_ref, *refs, nk, nj, tn, emit_x):
    if emit_x:
        xo_ref, h_ref, row_ref = refs
    else:
        h_ref, row_ref = refs
    k = pl.program_id(1)
    j = pl.program_id(2)

    def chunk(first, last):
        base = x_ref[...] if first else row_ref[j]
        val = base + jnp.dot(a_ref[...], w_ref[...], preferred_element_type=F32)
        row_ref[j] = val
        if last and emit_x:
            xo_ref[...] = val

    pl.when(k == 0)(functools.partial(chunk, True, nk == 1))
    if nk > 2:
        pl.when((k > 0) & (k < nk - 1))(functools.partial(chunk, False, False))
    if nk > 1:
        pl.when(k == nk - 1)(functools.partial(chunk, False, True))

    @pl.when((k == nk - 1) & (j == nj - 1))
    def _():
        ssq = jnp.zeros((row_ref.shape[1], 1), F32)
        for jj in range(nj):
            blk = row_ref[jj]
            ssq = ssq + jnp.sum(blk * blk, axis=-1, keepdims=True)
        inv = lax.rsqrt(ssq * (1.0 / (nj * tn)) + EPS)
        for jj in range(nj):
            sl = slice(jj * tn, (jj + 1) * tn)
            h_ref[:, sl] = (row_ref[jj] * inv * g_ref[:, sl]).astype(h_ref.dtype)


def matmul_cols_res_norm(a, w, li, x, g, tm, tn, nk, h_dtype, emit_x, name):
    m, kdim = a.shape
    n = w.shape[2]
    nj = n // tn
    tk = kdim // nk
    last = nk - 1
    out_specs = [pl.BlockSpec((tm, n), lambda i, k, j: (i, 0))]
    out_shape = [jax.ShapeDtypeStruct((m, n), h_dtype)]
    if emit_x:
        out_specs = [pl.BlockSpec((tm, tn), lambda i, k, j: (i, jnp.where(k == last, j, 0)))] + out_specs
        out_shape = [jax.ShapeDtypeStruct((m, n), F32)] + out_shape
    return pl.pallas_call(
        functools.partial(_mm_cols_res_norm_kernel, nk=nk, nj=nj, tn=tn, emit_x=emit_x),
        grid=(m // tm, nk, nj),
        in_specs=[pl.BlockSpec((tm, tk), lambda i, k, j: (i, k)),
                  pl.BlockSpec((None, tk, tn), lambda i, k, j: (li, k, j)),
                  pl.BlockSpec((tm, tn), lambda i, k, j: (i, jnp.where(k == 0, j, nj - 1))),
                  pl.BlockSpec((1, n), lambda i, k, j: (0, 0))],
        out_specs=out_specs,
        out_shape=out_shape,
        scratch_shapes=[pltpu.VMEM((nj, tm, tn), F32)],
        compiler_params=_params("parallel", "arbitrary", "arbitrary"),
        name=name,
    )(a, w, x, g.reshape(1, n))


CONV_HALO = 32
CONV_RB = 32


def _conv_kernel(a_ref, b_ref, w_ref, bias_ref, lng_ref, lnb_ref, o_ref, ext_ref, y_ref, *, tm):
    nslab = CONV_WIDTH // LANES
    t = pl.program_id(1)

    @pl.when(t == 0)
    def _():
        ext_ref[:, 0:CONV_HALO, :] = jnp.zeros((nslab, CONV_HALO, LANES), F32)

    for c in range(nslab):
        sl = slice(c * LANES, (c + 1) * LANES)
        ext_ref[c, CONV_HALO:CONV_HALO + tm, :] = a_ref[:, sl] * _sigmoid(b_ref[:, sl])

    first = CONV_HALO - (CONV_K - 1)
    for c in range(nslab):
        sl = slice(c * LANES, (c + 1) * LANES)

        def row_block(rb, carry, c=c, sl=sl):
            r0 = pl.multiple_of(rb * CONV_RB, CONV_RB)
            acc = [jnp.broadcast_to(bias_ref[:, sl], (CONV_RB, LANES)), None]
            for k in range(CONV_K):
                term = w_ref[k:k + 1, sl] * ext_ref[c, pl.ds(r0 + first + k, CONV_RB), :]
                acc[k % 2] = term if acc[k % 2] is None else acc[k % 2] + term
            y_ref[pl.ds(r0, CONV_RB), sl] = acc[0] + acc[1]
            return carry

        lax.fori_loop(0, tm // CONV_RB, row_block, 0, unroll=2)

    ext_ref[:, 0:CONV_HALO, :] = ext_ref[:, tm:tm + CONV_HALO, :]

    def ln_block(rb, carry):
        r0 = pl.multiple_of(rb * CONV_RB, CONV_RB)
        y = y_ref[pl.ds(r0, CONV_RB), :]
        mu = jnp.mean(y, axis=-1, keepdims=True)
        yc = y - mu
        yn = yc * lax.rsqrt(jnp.mean(yc * yc, axis=-1, keepdims=True) + EPS)
        yn = yn * lng_ref[...] + lnb_ref[...]
        o_ref[pl.ds(r0, CONV_RB), :] = _silu(yn).astype(o_ref.dtype)
        return carry

    lax.fori_loop(0, tm // CONV_RB, ln_block, 0, unroll=4)


def conv_branch(proj, dw_w, dw_b, ln_g, ln_b, bsz, seq, tm):
    nt = seq // tm
    cw = CONV_WIDTH
    row = lambda b, t: (b * nt + t, 0)
    vec = lambda b, t: (0, 0)
    return pl.pallas_call(
        functools.partial(_conv_kernel, tm=tm),
        grid=(bsz, nt),
        in_specs=[pl.BlockSpec((tm, cw), lambda b, t: (b * nt + t, 0)),
                  pl.BlockSpec((tm, cw), lambda b, t: (b * nt + t, 1)),
                  pl.BlockSpec((CONV_K, cw), vec),
                  pl.BlockSpec((1, cw), vec),
                  pl.BlockSpec((1, cw), vec),
                  pl.BlockSpec((1, cw), vec)],
        out_specs=pl.BlockSpec((tm, cw), row),
        out_shape=jax.ShapeDtypeStruct((bsz * seq, cw), BF16),
        scratch_shapes=[pltpu.VMEM((cw // LANES, CONV_HALO + tm, LANES), F32),
                        pltpu.VMEM((tm, cw), F32)],
        compiler_params=_params("parallel", "arbitrary"),
        name="conv_branch",
    )(proj, proj, dw_w, dw_b.reshape(1, cw), ln_g.reshape(1, cw), ln_b.reshape(1, cw))


SSM_GB = 16
SSM_UB = SSM_GB * SSM_GROUP
SSM_CB = SSM_GB * SSM_STATE
SSM_NJ = 2 * SSM_CB // LANES
SSM_PAD = 4


def _zoh(ar, ai, ldt):
    ar = jnp.minimum(ar, -1e-4)
    dt = jnp.exp(ldt)
    mag = jnp.exp(dt * ar)
    return mag * jnp.cos(dt * ai), mag * jnp.sin(dt * ai), ar


def _ssm_prep_kernel(arow_ref, avec_ref, bt_ref, ct_ref, abar_ref, bw_ref, cw_ref):
    are, aim, _ = _zoh(avec_ref[0], avec_ref[1], avec_ref[2])
    abar_ref[0] = are
    abar_ref[1] = aim

    abar_re, abar_im, ar = _zoh(arow_ref[0], arow_ref[1], arow_ref[2])
    ai = arow_ref[1]
    den = ar * ar + ai * ai
    nr = abar_re - 1.0
    z_re = (nr * ar + abar_im * ai) / den
    z_im = (abar_im * ar - nr * ai) / den
    br = bt_ref[0]
    bi = bt_ref[1]
    bbar = (z_re * br - z_im * bi, z_re * bi + z_im * br)

    rg = lax.broadcasted_iota(jnp.int32, (SSM_UB, SSM_CB), 0) // SSM_GROUP
    cg = lax.broadcasted_iota(jnp.int32, (SSM_UB, SSM_CB), 1) // SSM_STATE
    for part in range(2):
        tiled = jnp.concatenate([bbar[part]] * SSM_GB, axis=0)
        bw_ref[:, part * SSM_CB:(part + 1) * SSM_CB] = jnp.where(rg == cg, tiled, 0.0).astype(BF16)

    rg = lax.broadcasted_iota(jnp.int32, (SSM_CB, SSM_UB), 0) // SSM_STATE
    cg = lax.broadcasted_iota(jnp.int32, (SSM_CB, SSM_UB), 1) // SSM_GROUP
    for part, sign in ((0, 1.0), (1, -1.0)):
        tiled = jnp.concatenate([sign * ct_ref[part]] * SSM_GB, axis=0)
        cw_ref[part * SSM_CB:(part + 1) * SSM_CB, :] = jnp.where(rg == cg, tiled, 0.0).astype(BF16)


def ssm_prep(a_re, a_im, log_dt, b_re, b_im, c_re, c_im):
    nl = a_re.shape[0]
    ngb = SSM_GROUPS // SSM_GB
    nblk = nl * ngb
    ldt = jnp.broadcast_to(log_dt[:, :, None], a_re.shape)
    a3 = jnp.stack([a_re, a_im, ldt], axis=1).reshape(nl, 3, ngb, SSM_CB)
    a3 = jnp.transpose(a3, (0, 2, 1, 3)).reshape(nblk, 3, SSM_CB)
    arow = a3.reshape(nblk, 3, 1, SSM_CB)
    avec = a3.reshape(nblk, 3, SUBLANES, LANES)

    def b_t(b):
        v = b.reshape(nl, ngb, SSM_CB, SSM_GROUP)
        return jnp.transpose(v, (0, 1, 3, 2)).reshape(nblk, SSM_GROUP, SSM_CB)

    def c_t(c):
        v = c.reshape(nl, ngb, SSM_GB, SSM_GROUP, SSM_STATE)
        return jnp.transpose(v, (0, 1, 4, 2, 3)).reshape(nblk, SSM_STATE, SSM_UB)

    bt = jnp.stack([b_t(b_re), b_t(b_im)], axis=1)
    ct = jnp.stack([c_t(c_re), c_t(c_im)], axis=1)
    blk = lambda i: (i, 0, 0, 0)
    return pl.pallas_call(
        _ssm_prep_kernel,
        grid=(nblk,),
        in_specs=[pl.BlockSpec((None, 3, 1, SSM_CB), blk),
                  pl.BlockSpec((None, 3, SUBLANES, LANES), blk),
                  pl.BlockSpec((None, 2, SSM_GROUP, SSM_CB), blk),
                  pl.BlockSpec((None, 2, SSM_STATE, SSM_UB), blk)],
        out_specs=[pl.BlockSpec((None, 2, SUBLANES, LANES), blk),
                   pl.BlockSpec((None, SSM_UB, 2 * SSM_CB), lambda i: (i, 0, 0)),
                   pl.BlockSpec((None, 2 * SSM_CB, SSM_UB), lambda i: (i, 0, 0))],
        out_shape=[jax.ShapeDtypeStruct((nblk, 2, SUBLANES, LANES), F32),
                   jax.ShapeDtypeStruct((nblk, SSM_UB, 2 * SSM_CB), BF16),
                   jax.ShapeDtypeStruct((nblk, 2 * SSM_CB, SSM_UB), BF16)],
        compiler_params=_params("parallel"),
        name="ssm_prep",
    )(arow, avec, bt, ct)


def _ssm_kernel(u_ref, bw_ref, cw_ref, abar_ref, d_ref, o_ref, s_ref, st_ref, *, tt, nb, ng):
    pitch = tt + SSM_PAD
    half = SSM_NJ // 2
    chains = [(b, gl) for b in range(nb) for gl in range(ng)]
    t_idx = pl.program_id(1)

    @pl.when(t_idx == 0)
    def _():
        st_ref[...] = jnp.zeros(st_ref.shape, F32)

    for q, (b, gl) in enumerate(chains):
        ucols = slice(gl * SSM_UB, (gl + 1) * SSM_UB)
        bu = jnp.dot(u_ref[b, :, ucols].astype(BF16), bw_ref[gl], preferred_element_type=F32)
        for j in range(SSM_NJ):
            s_ref[q, j * pitch:j * pitch + tt, :] = bu[:, j * LANES:(j + 1) * LANES]

    ar = [abar_ref[gl, 0] for gl in range(ng)]
    ai = [abar_ref[gl, 1] for gl in range(ng)]

    def step(t, carry):
        out = []
        for q, (b, gl) in enumerate(chains):
            xr, xi = carry[2 * q], carry[2 * q + 1]
            sq = s_ref.at[q]
            br = sq[pl.ds(t, half, stride=pitch), :]
            bi = sq[pl.ds(half * pitch + t, half, stride=pitch), :]
            nr = ar[gl] * xr - ai[gl] * xi + br
            ni = ar[gl] * xi + ai[gl] * xr + bi
            sq[pl.ds(t, half, stride=pitch), :] = nr
            sq[pl.ds(half * pitch + t, half, stride=pitch), :] = ni
            out += [nr, ni]
        return tuple(out)

    init = tuple(st_ref[i] for i in range(2 * len(chains)))
    fin = lax.fori_loop(0, tt, step, init, unroll=8)
    for i in range(2 * len(chains)):
        st_ref[i] = fin[i]

    for q, (b, gl) in enumerate(chains):
        ucols = slice(gl * SSM_UB, (gl + 1) * SSM_UB)
        xs = jnp.concatenate([s_ref[q, j * pitch:j * pitch + tt, :].astype(BF16) for j in range(SSM_NJ)], axis=1)
        acc = jnp.dot(xs, cw_ref[gl], preferred_element_type=F32)
        y = acc + d_ref[:, ucols] * u_ref[b, :, ucols]
        o_ref[b, :, ucols] = jax.nn.gelu(y).astype(o_ref.dtype)


def ssm_branch(proj3, bw, cw, abar, d_skip, li, tt, ng):
    bsz, seq, _ = proj3.shape
    ngb = SSM_WIDTH // SSM_UB
    uw = ng * SSM_UB
    col0 = 2 * CONV_WIDTH // uw
    blk0 = li * ngb // ng
    pitch = tt + SSM_PAD
    return pl.pallas_call(
        functools.partial(_ssm_kernel, tt=tt, nb=bsz, ng=ng),
        grid=(ngb // ng, seq // tt),
        in_specs=[pl.BlockSpec((bsz, tt, uw), lambda g, t: (0, t, col0 + g)),
                  pl.BlockSpec((ng, SSM_UB, 2 * SSM_CB), lambda g, t: (blk0 + g, 0, 0)),
                  pl.BlockSpec((ng, 2 * SSM_CB, SSM_UB), lambda g, t: (blk0 + g, 0, 0)),
                  pl.BlockSpec((ng, 2, SUBLANES, LANES), lambda g, t: (blk0 + g, 0, 0, 0)),
                  pl.BlockSpec((None, 1, uw), lambda g, t: (li, 0, g))],
        out_specs=pl.BlockSpec((bsz, tt, uw), lambda g, t: (0, t, g)),
        out_shape=jax.ShapeDtypeStruct((bsz, seq, SSM_WIDTH), BF16),
        scratch_shapes=[pltpu.VMEM((bsz * ng, SSM_NJ * pitch, LANES), F32),
                        pltpu.VMEM((2 * bsz * ng, SUBLANES, LANES), F32)],
        compiler_params=_params("parallel", "arbitrary"),
        name="ssm_branch",
    )(proj3, bw, cw, abar, d_skip)


def _zmix_kernel(hc_ref, ys_ref, wpw_ref, wga_ref, wgb_ref, la_ref, lb_ref, z_ref, pwb_ref, gab_ref, gbb_ref):
    @pl.when(pl.program_id(1) == 0)
    def _():
        pwb_ref[...] = wpw_ref[...].astype(BF16)
        gab_ref[...] = wga_ref[...].astype(BF16)
        gbb_ref[...] = wgb_ref[...].astype(BF16)

    hc = hc_ref[...]
    ys = ys_ref[...]
    for c in range(z_ref.shape[1] // MXU_COLS):
        sl = slice(c * MXU_COLS, (c + 1) * MXU_COLS)
        ya = jnp.dot(hc, pwb_ref[:, sl], preferred_element_type=F32)
        ga = jnp.dot(ys, gab_ref[:, sl], preferred_element_type=F32)
        gb = jnp.dot(ys, gbb_ref[:, sl], preferred_element_type=F32)
        yb = ga * _sigmoid(gb)
        z = _sigmoid(la_ref[:, sl]) * ya + _sigmoid(lb_ref[:, sl]) * yb
        z_ref[:, sl] = z.astype(z_ref.dtype)


def zmix(hc, ys, w_pw, w_glu, li, proj, tm, tn):
    m = hc.shape[0]
    nj = D_MODEL // tn
    gate0 = (2 * CONV_WIDTH + SSM_WIDTH) // tn
    return pl.pallas_call(
        _zmix_kernel,
        grid=(nj, m // tm),
        in_specs=[pl.BlockSpec((tm, CONV_WIDTH), lambda j, i: (i, 0)),
                  pl.BlockSpec((tm, SSM_WIDTH), lambda j, i: (i, 0)),
                  pl.BlockSpec((None, CONV_WIDTH, tn), lambda j, i: (li, 0, j)),
                  pl.BlockSpec((None, SSM_WIDTH, tn), lambda j, i: (li, 0, j)),
                  pl.BlockSpec((None, SSM_WIDTH, tn), lambda j, i: (li, 0, nj + j)),
                  pl.BlockSpec((tm, tn), lambda j, i: (i, gate0 + j)),
                  pl.BlockSpec((tm, tn), lambda j, i: (i, gate0 + nj + j))],
        out_specs=pl.BlockSpec((tm, tn), lambda j, i: (i, j)),
        out_shape=jax.ShapeDtypeStruct((m, D_MODEL), BF16),
        scratch_shapes=[pltpu.VMEM((CONV_WIDTH, tn), BF16),
                        pltpu.VMEM((SSM_WIDTH, tn), BF16),
                        pltpu.VMEM((SSM_WIDTH, tn), BF16)],
        compiler_params=_params("parallel", "arbitrary"),
        name="zmix",
    )(hc, ys, w_pw, w_glu, w_glu, proj, proj)


def _attn_kernel(q_ref, k_ref, v_ref, o_ref):
    scale = XA_HEAD_DIM ** -0.5
    for h in range(XA_HEADS):
        sl = slice(h * XA_HEAD_DIM, (h + 1) * XA_HEAD_DIM)
        s = lax.dot_general(q_ref[:, sl], k_ref[:, sl], (((1,), (1,)), ((), ())),
                            preferred_element_type=F32) * scale
        s = s - jnp.max(s, axis=-1, keepdims=True)
        p = jnp.exp(s)
        p = p / jnp.sum(p, axis=-1, keepdims=True)
        o = jnp.dot(p.astype(BF16), v_ref[:, sl], preferred_element_type=F32)
        o_ref[:, sl] = o.astype(o_ref.dtype)


def attention(q, kv, bsz, seq, tm):
    nt = seq // tm
    return pl.pallas_call(
        _attn_kernel,
        grid=(bsz, nt),
        in_specs=[pl.BlockSpec((tm, D_MODEL), lambda b, t: (b * nt + t, 0)),
                  pl.BlockSpec((MEM_LEN, D_MODEL), lambda b, t: (b, 0)),
                  pl.BlockSpec((MEM_LEN, D_MODEL), lambda b, t: (b, 1))],
        out_specs=pl.BlockSpec((tm, D_MODEL), lambda b, t: (b * nt + t, 0)),
        out_shape=jax.ShapeDtypeStruct((bsz * seq, D_MODEL), BF16),
        compiler_params=_params("parallel", "arbitrary"),
        name="attention",
    )(q, kv, kv)


FFN_HALO = 8
FFN_RB = 32


def _ffn_up_kernel(h_ref, wg_ref, wv_ref, dg_ref, dv_ref, o_ref, wgb_ref, wvb_ref,
                   dgh_ref, eg_ref, ev_ref, *, tm, tn, nt):
    nslab = tn // LANES
    s = pl.program_id(1)

    @pl.when(s == 0)
    def _():
        wgb_ref[...] = wg_ref[...].astype(BF16)
        wvb_ref[...] = wv_ref[...].astype(BF16)
        dgh_ref[...] = 0.5 * dg_ref[...]

    @pl.when(s % nt == 0)
    def _():
        eg_ref[:, 0:FFN_HALO, :] = jnp.zeros((nslab, FFN_HALO, LANES), F32)
        ev_ref[:, 0:FFN_HALO, :] = jnp.zeros((nslab, FFN_HALO, LANES), F32)

    h = h_ref[...]
    for e_ref, wb_ref in ((eg_ref, wgb_ref), (ev_ref, wvb_ref)):
        for c2 in range(tn // MXU_COLS):
            u = jnp.dot(h, wb_ref[:, c2 * MXU_COLS:(c2 + 1) * MXU_COLS], preferred_element_type=F32)
            for half in range(MXU_COLS // LANES):
                c = c2 * (MXU_COLS // LANES) + half
                e_ref[c, FFN_HALO:FFN_HALO + tm, :] = u[:, half * LANES:(half + 1) * LANES]

    first = FFN_HALO - (FFN_K - 1)
    for c in range(nslab):
        sl = slice(c * LANES, (c + 1) * LANES)
        for rb in range(tm // FFN_RB):
            r0 = rb * FFN_RB
            hg = dgh_ref[0:1, sl] * eg_ref[c, first + r0:first + r0 + FFN_RB, :]
            v = dv_ref[0:1, sl] * ev_ref[c, first + r0:first + r0 + FFN_RB, :]
            for k in range(1, FFN_K):
                hg = hg + dgh_ref[k:k + 1, sl] * eg_ref[c, first + r0 + k:first + r0 + k + FFN_RB, :]
                v = v + dv_ref[k:k + 1, sl] * ev_ref[c, first + r0 + k:first + r0 + k + FFN_RB, :]
            o_ref[r0:r0 + FFN_RB, sl] = ((hg * jnp.tanh(hg) + hg) * v).astype(o_ref.dtype)

    eg_ref[:, 0:FFN_HALO, :] = eg_ref[:, tm:tm + FFN_HALO, :]
    ev_ref[:, 0:FFN_HALO, :] = ev_ref[:, tm:tm + FFN_HALO, :]


def ffn_up(h, w_up, dw_w, li, bsz, seq, tm, tn):
    nt = seq // tm
    nj = D_FF // tn
    slabs = (tn // LANES, FFN_HALO + tm, LANES)
    return pl.pallas_call(
        functools.partial(_ffn_up_kernel, tm=tm, tn=tn, nt=nt),
        grid=(nj, bsz * nt),
        in_specs=[pl.BlockSpec((tm, D_MODEL), lambda j, s: (s, 0)),
                  pl.BlockSpec((None, D_MODEL, tn), lambda j, s: (li, 0, j)),
                  pl.BlockSpec((None, D_MODEL, tn), lambda j, s: (li, 0, nj + j)),
                  pl.BlockSpec((None, FFN_K, tn), lambda j, s: (li, 0, j)),
                  pl.BlockSpec((None, FFN_K, tn), lambda j, s: (li, 0, nj + j))],
        out_specs=pl.BlockSpec((tm, tn), lambda j, s: (s, j)),
        out_shape=jax.ShapeDtypeStruct((bsz * seq, D_FF), BF16),
        scratch_shapes=[pltpu.VMEM((D_MODEL, tn), BF16),
                        pltpu.VMEM((D_MODEL, tn), BF16),
                        pltpu.VMEM((FFN_K, tn), F32),
                        pltpu.VMEM(slabs, F32),
                        pltpu.VMEM(slabs, F32)],
        compiler_params=_params("parallel", "arbitrary"),
        name="ffn_up",
    )(h, w_up, w_up, dw_w, dw_w)


def kernel(x, mem, mix_norm_g, w_in, conv_dw_w, conv_dw_b, conv_ln_g, conv_ln_b, conv_w_pw,
           ssm_a_re, ssm_a_im, ssm_log_dt, ssm_b_re, ssm_b_im, ssm_c_re, ssm_c_im, ssm_d,
           ssm_w_glu, w_out, xa_norm_g, mem_norm_g, xa_w_q, xa_w_kv, xa_w_o,
           ffn_norm_g, ffn_w_up, ffn_dw_w, ffn_w_down, final_norm_g):
    bsz, seq, d = x.shape
    depth = w_in.shape[0]
    rows = bsz * seq
    xf = x.reshape(rows, d)
    memf = mem.reshape(bsz * MEM_LEN, d)

    w_down_b = ffn_w_down.astype(BF16)
    abar, bw, cw = ssm_prep(ssm_a_re, ssm_a_im, ssm_log_dt, ssm_b_re, ssm_b_im, ssm_c_re, ssm_c_im)
    d_skip = ssm_d.reshape(depth, 1, SSM_WIDTH)

    h = rmsnorm_rows(xf, mix_norm_g[0], 512, BF16)
    out = None
    for i in range(depth):
        proj = matmul_wcast(h, w_in, i, 1024, 1024, F32, "in_proj")
        hc = conv_branch(proj, conv_dw_w[i], conv_dw_b[i], conv_ln_g[i], conv_ln_b[i], bsz, seq, 512)
        ys = ssm_branch(proj.reshape(bsz, seq, -1), bw, cw, abar, d_skip, i, 512, 2).reshape(rows, SSM_WIDTH)
        z = zmix(hc, ys, conv_w_pw, ssm_w_glu, i, proj, 1024, 512)
        xf, h = matmul_res_norm(z, w_out, i, xf, xa_norm_g[i], 512, BF16, "out_proj")
        q = matmul_wcast(h, xa_w_q, i, 1024, 1024, BF16, "q_proj")
        mn = rmsnorm_rows(memf, mem_norm_g[i], 256, BF16)
        kv = matmul_wcast(mn, xa_w_kv, i, 512, 1024, BF16, "kv_proj")
        o = attention(q, kv, bsz, seq, 1024)
        xf, h = matmul_res_norm(o, xa_w_o, i, xf, ffn_norm_g[i], 512, BF16, "o_proj")
        act = ffn_up(h, ffn_w_up, ffn_dw_w, i, bsz, seq, 1024, 512)
        if i + 1 < depth:
            xf, h = matmul_cols_res_norm(act, w_down_b, i, xf, mix_norm_g[i + 1],
                                         1024, 512, 2, BF16, True, "down_proj")
        else:
            out = matmul_cols_res_norm(act, w_down_b, i, xf, final_norm_g,
                                       1024, 512, 2, F32, False, "down_proj_final")[0]
    return out.reshape(bsz, seq, d)
```

```python
import functools
import math

import jax
import jax.numpy as jnp
from jax import lax
from jax.experimental import pallas as pl
from jax.experimental.pallas import tpu as pltpu

D_MODEL = 2048
MEM_LEN = 256
CONV_WIDTH = 1024
CONV_K = 31
SSM_WIDTH = 1024
SSM_GROUP = 16
SSM_GROUPS = SSM_WIDTH // SSM_GROUP
SSM_STATE = 64
XA_HEADS = 4
XA_HEAD_DIM = D_MODEL // XA_HEADS
D_FF = 5632
FFN_K = 3
EPS = 1e-6

LANES = 128
SUBLANES = 8
MXU_COLS = 256
VMEM_LIMIT = 56 * 1024 * 1024

BF16 = jnp.bfloat16
F32 = jnp.float32


def _params(*sem):
    return pltpu.CompilerParams(dimension_semantics=sem, vmem_limit_bytes=VMEM_LIMIT)


def _rms(x, g):
    return x * lax.rsqrt(jnp.mean(x * x, axis=-1, keepdims=True) + EPS) * g


def _sigmoid(x):
    return 0.5 * jnp.tanh(0.5 * x) + 0.5


def _silu(x):
    hx = 0.5 * x
    return hx * jnp.tanh(hx) + hx


def _rmsnorm_kernel(x_ref, g_ref, o_ref):
    o_ref[...] = _rms(x_ref[...], g_ref[...]).astype(o_ref.dtype)


def rmsnorm_rows(x, g, tm, out_dtype):
    m, d = x.shape
    return pl.pallas_call(
        _rmsnorm_kernel,
        grid=(m // tm,),
        in_specs=[pl.BlockSpec((tm, d), lambda i: (i, 0)),
                  pl.BlockSpec((1, d), lambda i: (0, 0))],
        out_specs=pl.BlockSpec((tm, d), lambda i: (i, 0)),
        out_shape=jax.ShapeDtypeStruct((m, d), out_dtype),
        compiler_params=_params("parallel"),
        name="rmsnorm",
    )(x, g.reshape(1, d))


def _mm_kernel(a_ref, w_ref, o_ref):
    o_ref[...] = jnp.dot(a_ref[...], w_ref[...], preferred_element_type=F32).astype(o_ref.dtype)


def matmul(a, w, li, tm, tn, out_dtype, name):
    m, k = a.shape
    n = w.shape[2]
    return pl.pallas_call(
        _mm_kernel,
        grid=(m // tm, n // tn),
        in_specs=[pl.BlockSpec((tm, k), lambda i, j: (i, 0)),
                  pl.BlockSpec((None, k, tn), lambda i, j: (li, 0, j))],
        out_specs=pl.BlockSpec((tm, tn), lambda i, j: (i, j)),
        out_shape=jax.ShapeDtypeStruct((m, n), out_dtype),
        compiler_params=_params("parallel", "arbitrary"),
        name=name,
    )(a, w)


def _mm_wcast_kernel(a_ref, w_ref, o_ref, wb_ref):
    @pl.when(pl.program_id(1) == 0)
    def _():
        wb_ref[...] = w_ref[...].astype(BF16)

    o_ref[...] = jnp.dot(a_ref[...], wb_ref[...], preferred_element_type=F32).astype(o_ref.dtype)


def matmul_wcast(a, w, li, tm, tn, out_dtype, name):
    m, k = a.shape
    n = w.shape[2]
    return pl.pallas_call(
        _mm_wcast_kernel,
        grid=(n // tn, m // tm),
        in_specs=[pl.BlockSpec((tm, k), lambda j, i: (i, 0)),
                  pl.BlockSpec((None, k, tn), lambda j, i: (li, 0, j))],
        out_specs=pl.BlockSpec((tm, tn), lambda j, i: (i, j)),
        out_shape=jax.ShapeDtypeStruct((m, n), out_dtype),
        scratch_shapes=[pltpu.VMEM((k, tn), BF16)],
        compiler_params=_params("parallel", "arbitrary"),
        name=name,
    )(a, w)


def _mm_res_norm_kernel(a_ref, w_ref, x_ref, g_ref, xo_ref, h_ref, wb_ref):
    @pl.when(pl.program_id(0) == 0)
    def _():
        wb_ref[...] = w_ref[...].astype(BF16)

    xn = x_ref[...] + jnp.dot(a_ref[...], wb_ref[...], preferred_element_type=F32)
    xo_ref[...] = xn
    h_ref[...] = _rms(xn, g_ref[...]).astype(h_ref.dtype)


def matmul_res_norm(a, w, li, x, g, tm, h_dtype, name):
    m, kdim = a.shape
    n = w.shape[2]
    row = lambda i: (i, 0)
    return pl.pallas_call(
        _mm_res_norm_kernel,
        grid=(m // tm,),
        in_specs=[pl.BlockSpec((tm, kdim), row),
                  pl.BlockSpec((None, kdim, n), lambda i: (li, 0, 0), pipeline_mode=pl.Buffered(1)),
                  pl.BlockSpec((tm, n), row),
                  pl.BlockSpec((1, n), lambda i: (0, 0))],
        out_specs=[pl.BlockSpec((tm, n), row), pl.BlockSpec((tm, n), row)],
        out_shape=[jax.ShapeDtypeStruct((m, n), F32), jax.ShapeDtypeStruct((m, n), h_dtype)],
        scratch_shapes=[pltpu.VMEM((kdim, n), BF16)],
        compiler_params=_params("arbitrary"),
        name=name,
    )(a, w, x, g.reshape(1, n))


def _mm_cols_res_norm_kernel(a_ref, w_ref, x_ref, g_ref, *refs, nk, nj, tn, emit_x):
    if emit_x:
        xo_ref, h_ref, row_ref = refs
    else:
        h_ref, row_ref = refs
    k = pl.program_id(1)
    j = pl.program_id(2)

    def chunk(first, last):
        base = x_ref[...] if first else row_ref[j]
        val = base + jnp.dot(a_ref[...], w_ref[...], preferred_element_type=F32)
        row_ref[j] = val
        if last and emit_x:
            xo_ref[...] = val

    pl.when(k == 0)(functools.partial(chunk, True, nk == 1))
    if nk > 2:
        pl.when((k > 0) & (k < nk - 1))(functools.partial(chunk, False, False))
    if nk > 1:
        pl.when(k == nk - 1)(functools.partial(chunk, False, True))

    @pl.when((k == nk - 1) & (j == nj - 1))
    def _():
        ssq = jnp.zeros((row_ref.shape[1], 1), F32)
        for jj in range(nj):
            blk = row_ref[jj]
            ssq = ssq + jnp.sum(blk * blk, axis=-1, keepdims=True)
        inv = lax.rsqrt(ssq * (1.0 / (nj * tn)) + EPS)
        for jj in range(nj):
            sl = slice(jj * tn, (jj + 1) * tn)
            h_ref[:, sl] = (row_ref[jj] * inv * g_ref[:, sl]).astype(h_ref.dtype)


def matmul_cols_res_norm(a, w, li, x, g, tm, tn, nk, h_dtype, emit_x, name):
    m, kdim = a.shape
    n = w.shape[2]
    nj = n // tn
    tk = kdim // nk
    last = nk - 1
    out_specs = [pl.BlockSpec((tm, n), lambda i, k, j: (i, 0))]
    out_shape = [jax.ShapeDtypeStruct((m, n), h_dtype)]
    if emit_x:
        out_specs = [pl.BlockSpec((tm, tn), lambda i, k, j: (i, jnp.where(k == last, j, 0)))] + out_specs
        out_shape = [jax.ShapeDtypeStruct((m, n), F32)] + out_shape
    return pl.pallas_call(
        functools.partial(_mm_cols_res_norm_kernel, nk=nk, nj=nj, tn=tn, emit_x=emit_x),
        grid=(m // tm, nk, nj),
        in_specs=[pl.BlockSpec((tm, tk), lambda i, k, j: (i, k)),
                  pl.BlockSpec((None, tk, tn), lambda i, k, j: (li, k, j)),
                  pl.BlockSpec((tm, tn), lambda i, k, j: (i, jnp.where(k == 0, j, nj - 1))),
                  pl.BlockSpec((1, n), lambda i, k, j: (0, 0))],
        out_specs=out_specs,
        out_shape=out_shape,
        scratch_shapes=[pltpu.VMEM((nj, tm, tn), F32)],
        compiler_params=_params("parallel", "arbitrary", "arbitrary"),
        name=name,
    )(a, w, x, g.reshape(1, n))


CONV_HALO = 32
CONV_RB = 32


def _conv_kernel(a_ref, b_ref, w_ref, bias_ref, lng_ref, lnb_ref, o_ref, ext_ref, y_ref, *, tm):
    nslab = CONV_WIDTH // LANES
    t = pl.program_id(1)

    @pl.when(t == 0)
    def _():
        ext_ref[:, 0:CONV_HALO, :] = jnp.zeros((nslab, CONV_HALO, LANES), F32)

    for c in range(nslab):
        sl = slice(c * LANES, (c + 1) * LANES)
        ext_ref[c, CONV_HALO:CONV_HALO + tm, :] = a_ref[:, sl] * _sigmoid(b_ref[:, sl])

    first = CONV_HALO - (CONV_K - 1)
    for c in range(nslab):
        sl = slice(c * LANES, (c + 1) * LANES)

        def row_block(rb, carry, c=c, sl=sl):
            r0 = pl.multiple_of(rb * CONV_RB, CONV_RB)
            acc = [jnp.broadcast_to(bias_ref[:, sl], (CONV_RB, LANES)), None]
            for k in range(CONV_K):
                term = w_ref[k:k + 1, sl] * ext_ref[c, pl.ds(r0 + first + k, CONV_RB), :]
                acc[k % 2] = term if acc[k % 2] is None else acc[k % 2] + term
            y_ref[pl.ds(r0, CONV_RB), sl] = acc[0] + acc[1]
            return carry

        lax.fori_loop(0, tm // CONV_RB, row_block, 0, unroll=2)

    ext_ref[:, 0:CONV_HALO, :] = ext_ref[:, tm:tm + CONV_HALO, :]

    def ln_block(rb, carry):
        r0 = pl.multiple_of(rb * CONV_RB, CONV_RB)
        y = y_ref[pl.ds(r0, CONV_RB), :]
        mu = jnp.mean(y, axis=-1, keepdims=True)
        yc = y - mu
        yn = yc * lax.rsqrt(jnp.mean(yc * yc, axis=-1, keepdims=True) + EPS)
        yn = yn * lng_ref[...] + lnb_ref[...]
        o_ref[pl.ds(r0, CONV_RB), :] = _silu(yn).astype(o_ref.dtype)
        return carry

    lax.fori_loop(0, tm // CONV_RB, ln_block, 0, unroll=4)


def conv_branch(proj, dw_w, dw_b, ln_g, ln_b, bsz, seq, tm):
    nt = seq // tm
    cw = CONV_WIDTH
    row = lambda b, t: (b * nt + t, 0)
    vec = lambda b, t: (0, 0)
    return pl.pallas_call(
        functools.partial(_conv_kernel, tm=tm),
        grid=(bsz, nt),
        in_specs=[pl.BlockSpec((tm, cw), lambda b, t: (b * nt + t, 0)),
                  pl.BlockSpec((tm, cw), lambda b, t: (b * nt + t, 1)),
                  pl.BlockSpec((CONV_K, cw), vec),
                  pl.BlockSpec((1, cw), vec),
                  pl.BlockSpec((1, cw), vec),
                  pl.BlockSpec((1, cw), vec)],
        out_specs=pl.BlockSpec((tm, cw), row),
        out_shape=jax.ShapeDtypeStruct((bsz * seq, cw), BF16),
        scratch_shapes=[pltpu.VMEM((cw // LANES, CONV_HALO + tm, LANES), F32),
                        pltpu.VMEM((tm, cw), F32)],
        compiler_params=_params("parallel", "arbitrary"),
        name="conv_branch",
    )(proj, proj, dw_w, dw_b.reshape(1, cw), ln_g.reshape(1, cw), ln_b.reshape(1, cw))


SSM_GB = 16
SSM_UB = SSM_GB * SSM_GROUP
SSM_CB = SSM_GB * SSM_STATE
SSM_NJ = 2 * SSM_CB // LANES
SSM_PAD = 4


def _zoh(ar, ai, ldt):
    ar = jnp.minimum(ar, -1e-4)
    dt = jnp.exp(ldt)
    mag = jnp.exp(dt * ar)
    return mag * jnp.cos(dt * ai), mag * jnp.sin(dt * ai), ar


def _ssm_prep_kernel(arow_ref, avec_ref, bt_ref, ct_ref, abar_ref, bw_ref, cw_ref):
    are, aim, _ = _zoh(avec_ref[0], avec_ref[1], avec_ref[2])
    abar_ref[0] = are
    abar_ref[1] = aim

    abar_re, abar_im, ar = _zoh(arow_ref[0], arow_ref[1], arow_ref[2])
    ai = arow_ref[1]
    den = ar * ar + ai * ai
    nr = abar_re - 1.0
    z_re = (nr * ar + abar_im * ai) / den
    z_im = (abar_im * ar - nr * ai) / den
    br = bt_ref[0]
    bi = bt_ref[1]
    bbar = (z_re * br - z_im * bi, z_re * bi + z_im * br)

    rg = lax.broadcasted_iota(jnp.int32, (SSM_UB, SSM_CB), 0) // SSM_GROUP
    cg = lax.broadcasted_iota(jnp.int32, (SSM_UB, SSM_CB), 1) // SSM_STATE
    for part in range(2):
        tiled = jnp.concatenate([bbar[part]] * SSM_GB, axis=0)
        bw_ref[:, part * SSM_CB:(part + 1) * SSM_CB] = jnp.where(rg == cg, tiled, 0.0).astype(BF16)

    rg = lax.broadcasted_iota(jnp.int32, (SSM_CB, SSM_UB), 0) // SSM_STATE
    cg = lax.broadcasted_iota(jnp.int32, (SSM_CB, SSM_UB), 1) // SSM_GROUP
    for part, sign in ((0, 1.0), (1, -1.0)):
        tiled = jnp.concatenate([sign * ct_ref[part]] * SSM_GB, axis=0)
        cw_ref[part * SSM_CB:(part + 1) * SSM_CB, :] = jnp.where(rg == cg, tiled, 0.0).astype(BF16)


def ssm_prep(a_re, a_im, log_dt, b_re, b_im, c_re, c_im):
    nl = a_re.shape[0]
    ngb = SSM_GROUPS // SSM_GB
    nblk = nl * ngb
    ldt = jnp.broadcast_to(log_dt[:, :, None], a_re.shape)
    a3 = jnp.stack([a_re, a_im, ldt], axis=1).reshape(nl, 3, ngb, SSM_CB)
    a3 = jnp.transpose(a3, (0, 2, 1, 3)).reshape(nblk, 3, SSM_CB)
    arow = a3.reshape(nblk, 3, 1, SSM_CB)
    avec = a3.reshape(nblk, 3, SUBLANES, LANES)

    def b_t(b):
        v = b.reshape(nl, ngb, SSM_CB, SSM_GROUP)
        return jnp.transpose(v, (0, 1, 3, 2)).reshape(nblk, SSM_GROUP, SSM_CB)

    def c_t(c):
        v = c.reshape(nl, ngb, SSM_GB, SSM_GROUP, SSM_STATE)
        return jnp.transpose(v, (0, 1, 4, 2, 3)).reshape(nblk, SSM_STATE, SSM_UB)

    bt = jnp.stack([b_t(b_re), b_t(b_im)], axis=1)
    ct = jnp.stack([c_t(c_re), c_t(c_im)], axis=1)
    blk = lambda i: (i, 0, 0, 0)
    return pl.pallas_call(
        _ssm_prep_kernel,
        grid=(nblk,),
        in_specs=[pl.BlockSpec((None, 3, 1, SSM_CB), blk),
                  pl.BlockSpec((None, 3, SUBLANES, LANES), blk),
                  pl.BlockSpec((None, 2, SSM_GROUP, SSM_CB), blk),
                  pl.BlockSpec((None, 2, SSM_STATE, SSM_UB), blk)],
        out_specs=[pl.BlockSpec((None, 2, SUBLANES, LANES), blk),
                   pl.BlockSpec((None, SSM_UB, 2 * SSM_CB), lambda i: (i, 0, 0)),
                   pl.BlockSpec((None, 2 * SSM_CB, SSM_UB), lambda i: (i, 0, 0))],
        out_shape=[jax.ShapeDtypeStruct((nblk, 2, SUBLANES, LANES), F32),
                   jax.ShapeDtypeStruct((nblk, SSM_UB, 2 * SSM_CB), BF16),
                   jax.ShapeDtypeStruct((nblk, 2 * SSM_CB, SSM_UB), BF16)],
        compiler_params=_params("parallel"),
        name="ssm_prep",
    )(arow, avec, bt, ct)


def _ssm_kernel(u_ref, bw_ref, cw_ref, abar_ref, d_ref, o_ref, s_ref, st_ref, *, tt, nb, ng):
    pitch = tt + SSM_PAD
    half = SSM_NJ // 2
    chains = [(b, gl) for b in range(nb) for gl in range(ng)]
    t_idx = pl.program_id(1)

    @pl.when(t_idx == 0)
    def _():
        st_ref[...] = jnp.zeros(st_ref.shape, F32)

    for q, (b, gl) in enumerate(chains):
        ucols = slice(gl * SSM_UB, (gl + 1) * SSM_UB)
        bu = jnp.dot(u_ref[b, :, ucols].astype(BF16), bw_ref[gl], preferred_element_type=F32)
        for j in range(SSM_NJ):
            s_ref[q, j * pitch:j * pitch + tt, :] = bu[:, j * LANES:(j + 1) * LANES]

    ar = [abar_ref[gl, 0] for gl in range(ng)]
    ai = [abar_ref[gl, 1] for gl in range(ng)]

    for gl in range(ng):
        qs = [b * ng + gl for b in range(nb)]
        state = [(st_ref[2 * q], st_ref[2 * q + 1]) for q in qs]
        for t in range(tt):
            for idx, q in enumerate(qs):
                xr, xi = state[idx]
                sq = s_ref.at[q]
                br = sq[pl.ds(t, half, stride=pitch), :]
                bi = sq[pl.ds(half * pitch + t, half, stride=pitch), :]
                nr = ar[gl] * xr - ai[gl] * xi + br
                ni = ar[gl] * xi + ai[gl] * xr + bi
                sq[pl.ds(t, half, stride=pitch), :] = nr
                sq[pl.ds(half * pitch + t, half, stride=pitch), :] = ni
                state[idx] = (nr, ni)
        for idx, q in enumerate(qs):
            st_ref[2 * q] = state[idx][0]
            st_ref[2 * q + 1] = state[idx][1]

    for q, (b, gl) in enumerate(chains):
        ucols = slice(gl * SSM_UB, (gl + 1) * SSM_UB)
        xs = jnp.concatenate([s_ref[q, j * pitch:j * pitch + tt, :].astype(BF16) for j in range(SSM_NJ)], axis=1)
        acc = jnp.dot(xs, cw_ref[gl], preferred_element_type=F32)
        y = acc + d_ref[:, ucols] * u_ref[b, :, ucols]
        o_ref[b, :, ucols] = jax.nn.gelu(y).astype(o_ref.dtype)


def ssm_branch(proj3, bw, cw, abar, d_skip, li, tt, ng):
    bsz, seq, _ = proj3.shape
    ngb = SSM_WIDTH // SSM_UB
    uw = ng * SSM_UB
    col0 = 2 * CONV_WIDTH // uw
    blk0 = li * ngb // ng
    pitch = tt + SSM_PAD
    return pl.pallas_call(
        functools.partial(_ssm_kernel, tt=tt, nb=bsz, ng=ng),
        grid=(ngb // ng, seq // tt),
        in_specs=[pl.BlockSpec((bsz, tt, uw), lambda g, t: (0, t, col0 + g)),
                  pl.BlockSpec((ng, SSM_UB, 2 * SSM_CB), lambda g, t: (blk0 + g, 0, 0)),
                  pl.BlockSpec((ng, 2 * SSM_CB, SSM_UB), lambda g, t: (blk0 + g, 0, 0)),
                  pl.BlockSpec((ng, 2, SUBLANES, LANES), lambda g, t: (blk0 + g, 0, 0, 0)),
                  pl.BlockSpec((None, 1, uw), lambda g, t: (li, 0, g))],
        out_specs=pl.BlockSpec((bsz, tt, uw), lambda g, t: (0, t, g)),
        out_shape=jax.ShapeDtypeStruct((bsz, seq, SSM_WIDTH), BF16),
        scratch_shapes=[pltpu.VMEM((bsz * ng, SSM_NJ * pitch, LANES), F32),
                        pltpu.VMEM((2 * bsz * ng, SUBLANES, LANES), F32)],
        compiler_params=_params("parallel", "arbitrary"),
        name="ssm_branch",
    )(proj3, bw, cw, abar, d_skip)


def _zmix_kernel(hc_ref, ys_ref, wpw_ref, wga_ref, wgb_ref, la_ref, lb_ref, z_ref, pwb_ref, gab_ref, gbb_ref):
    @pl.when(pl.program_id(1) == 0)
    def _():
        pwb_ref[...] = wpw_ref[...].astype(BF16)
        gab_ref[...] = wga_ref[...].astype(BF16)
        gbb_ref[...] = wgb_ref[...].astype(BF16)

    hc = hc_ref[...]
    ys = ys_ref[...]
    for c in range(z_ref.shape[1] // MXU_COLS):
        sl = slice(c * MXU_COLS, (c + 1) * MXU_COLS)
        ya = jnp.dot(hc, pwb_ref[:, sl], preferred_element_type=F32)
        ga = jnp.dot(ys, gab_ref[:, sl], preferred_element_type=F32)
        gb = jnp.dot(ys, gbb_ref[:, sl], preferred_element_type=F32)
        yb = ga * _sigmoid(gb)
        z = _sigmoid(la_ref[:, sl]) * ya + _sigmoid(lb_ref[:, sl]) * yb
        z_ref[:, sl] = z.astype(z_ref.dtype)


def zmix(hc, ys, w_pw, w_glu, li, proj, tm, tn):
    m = hc.shape[0]
    nj = D_MODEL // tn
    gate0 = (2 * CONV_WIDTH + SSM_WIDTH) // tn
    return pl.pallas_call(
        _zmix_kernel,
        grid=(nj, m // tm),
        in_specs=[pl.BlockSpec((tm, CONV_WIDTH), lambda j, i: (i, 0)),
                  pl.BlockSpec((tm, SSM_WIDTH), lambda j, i: (i, 0)),
                  pl.BlockSpec((None, CONV_WIDTH, tn), lambda j, i: (li, 0, j)),
                  pl.BlockSpec((None, SSM_WIDTH, tn), lambda j, i: (li, 0, j)),
                  pl.BlockSpec((None, SSM_WIDTH, tn), lambda j, i: (li, 0, nj + j)),
                  pl.BlockSpec((tm, tn), lambda j, i: (i, gate0 + j)),
                  pl.BlockSpec((tm, tn), lambda j, i: (i, gate0 + nj + j))],
        out_specs=pl.BlockSpec((tm, tn), lambda j, i: (i, j)),
        out_shape=jax.ShapeDtypeStruct((m, D_MODEL), BF16),
        scratch_shapes=[pltpu.VMEM((CONV_WIDTH, tn), BF16),
                        pltpu.VMEM((SSM_WIDTH, tn), BF16),
                        pltpu.VMEM((SSM_WIDTH, tn), BF16)],
        compiler_params=_params("parallel", "arbitrary"),
        name="zmix",
    )(hc, ys, w_pw, w_glu, w_glu, proj, proj)


def _attn_kernel(q_ref, k_ref, v_ref, o_ref):
    scale = XA_HEAD_DIM ** -0.5
    for h in range(XA_HEADS):
        sl = slice(h * XA_HEAD_DIM, (h + 1) * XA_HEAD_DIM)
        s = lax.dot_general(q_ref[:, sl], k_ref[:, sl], (((1,), (1,)), ((), ())),
                            preferred_element_type=F32) * scale
        s = s - jnp.max(s, axis=-1, keepdims=True)
        p = jnp.exp(s)
        p = p / jnp.sum(p, axis=-1, keepdims=True)
        o = jnp.dot(p.astype(BF16), v_ref[:, sl], preferred_element_type=F32)
        o_ref[:, sl] = o.astype(o_ref.dtype)


def attention(q, kv, bsz, seq, tm):
    nt = seq // tm
    return pl.pallas_call(
        _attn_kernel,
        grid=(bsz, nt),
        in_specs=[pl.BlockSpec((tm, D_MODEL), lambda b, t: (b * nt + t, 0)),
                  pl.BlockSpec((MEM_LEN, D_MODEL), lambda b, t: (b, 0)),
                  pl.BlockSpec((MEM_LEN, D_MODEL), lambda b, t: (b, 1))],
        out_specs=pl.BlockSpec((tm, D_MODEL), lambda b, t: (b * nt + t, 0)),
        out_shape=jax.ShapeDtypeStruct((bsz * seq, D_MODEL), BF16),
        compiler_params=_params("parallel", "arbitrary"),
        name="attention",
    )(q, kv, kv)


FFN_HALO = 8
FFN_RB = 32


def _ffn_up_kernel(h_ref, wg_ref, wv_ref, dg_ref, dv_ref, o_ref, wgb_ref, wvb_ref,
                   dgh_ref, eg_ref, ev_ref, *, tm, tn, nt):
    nslab = tn // LANES
    s = pl.program_id(1)

    @pl.when(s == 0)
    def _():
        wgb_ref[...] = wg_ref[...].astype(BF16)
        wvb_ref[...] = wv_ref[...].astype(BF16)
        dgh_ref[...] = 0.5 * dg_ref[...]

    @pl.when(s % nt == 0)
    def _():
        eg_ref[:, 0:FFN_HALO, :] = jnp.zeros((nslab, FFN_HALO, LANES), F32)
        ev_ref[:, 0:FFN_HALO, :] = jnp.zeros((nslab, FFN_HALO, LANES), F32)

    h = h_ref[...]
    for e_ref, wb_ref in ((eg_ref, wgb_ref), (ev_ref, wvb_ref)):
        for c2 in range(tn // MXU_COLS):
            u = jnp.dot(h, wb_ref[:, c2 * MXU_COLS:(c2 + 1) * MXU_COLS], preferred_element_type=F32)
            for half in range(MXU_COLS // LANES):
                c = c2 * (MXU_COLS // LANES) + half
                e_ref[c, FFN_HALO:FFN_HALO + tm, :] = u[:, half * LANES:(half + 1) * LANES]

    first = FFN_HALO - (FFN_K - 1)
    for c in range(nslab):
        sl = slice(c * LANES, (c + 1) * LANES)
        for rb in range(tm // FFN_RB):
            r0 = rb * FFN_RB
            hg = dgh_ref[0:1, sl] * eg_ref[c, first + r0:first + r0 + FFN_RB, :]
            v = dv_ref[0:1, sl] * ev_ref[c, first + r0:first + r0 + FFN_RB, :]
            for k in range(1, FFN_K):
                hg = hg + dgh_ref[k:k + 1, sl] * eg_ref[c, first + r0 + k:first + r0 + k + FFN_RB, :]
                v = v + dv_ref[k:k + 1, sl] * ev_ref[c, first + r0 + k:first + r0 + k + FFN_RB, :]
            o_ref[r0:r0 + FFN_RB, sl] = ((hg * jnp.tanh(hg) + hg) * v).astype(o_ref.dtype)

    eg_ref[:, 0:FFN_HALO, :] = eg_ref[:, tm:tm + FFN_HALO, :]
    ev_ref[:, 0:FFN_HALO, :] = ev_ref[:, tm:tm + FFN_HALO, :]


def ffn_up(h, w_up, dw_w, li, bsz, seq, tm, tn):
    nt = seq // tm
    nj = D_FF // tn
    slabs = (tn // LANES, FFN_HALO + tm, LANES)
    return pl.pallas_call(
        functools.partial(_ffn_up_kernel, tm=tm, tn=tn, nt=nt),
        grid=(nj, bsz * nt),
        in_specs=[pl.BlockSpec((tm, D_MODEL), lambda j, s: (s, 0)),
                  pl.BlockSpec((None, D_MODEL, tn), lambda j, s: (li, 0, j)),
                  pl.BlockSpec((None, D_MODEL, tn), lambda j, s: (li, 0, nj + j)),
                  pl.BlockSpec((None, FFN_K, tn), lambda j, s: (li, 0, j)),
                  pl.BlockSpec((None, FFN_K, tn), lambda j, s: (li, 0, nj + j))],
        out_specs=pl.BlockSpec((tm, tn), lambda j, s: (s, j)),
        out_shape=jax.ShapeDtypeStruct((bsz * seq, D_FF), BF16),
        scratch_shapes=[pltpu.VMEM((D_MODEL, tn), BF16),
                        pltpu.VMEM((D_MODEL, tn), BF16),
                        pltpu.VMEM((FFN_K, tn), F32),
                        pltpu.VMEM(slabs, F32),
                        pltpu.VMEM(slabs, F32)],
        compiler_params=_params("parallel", "arbitrary"),
        name="ffn_up",
    )(h, w_up, w_up, dw_w, dw_w)


def kernel(x, mem, mix_norm_g, w_in, conv_dw_w, conv_dw_b, conv_ln_g, conv_ln_b, conv_w_pw,
           ssm_a_re, ssm_a_im, ssm_log_dt, ssm_b_re, ssm_b_im, ssm_c_re, ssm_c_im, ssm_d,
           ssm_w_glu, w_out, xa_norm_g, mem_norm_g, xa_w_q, xa_w_kv, xa_w_o,
           ffn_norm_g, ffn_w_up, ffn_dw_w, ffn_w_down, final_norm_g):
    bsz, seq, d = x.shape
    depth = w_in.shape[0]
    rows = bsz * seq
    xf = x.reshape(rows, d)
    memf = mem.reshape(bsz * MEM_LEN, d)

    w_down_b = ffn_w_down.astype(BF16)
    abar, bw, cw = ssm_prep(ssm_a_re, ssm_a_im, ssm_log_dt, ssm_b_re, ssm_b_im, ssm_c_re, ssm_c_im)
    d_skip = ssm_d.reshape(depth, 1, SSM_WIDTH)

    h = rmsnorm_rows(xf, mix_norm_g[0], 512, BF16)
    out = None
    for i in range(depth):
        proj = matmul_wcast(h, w_in, i, 1024, 1024, F32, "in_proj")
        hc = conv_branch(proj, conv_dw_w[i], conv_dw_b[i], conv_ln_g[i], conv_ln_b[i], bsz, seq, 512)
        ys = ssm_branch(proj.reshape(bsz, seq, -1), bw, cw, abar, d_skip, i, 512, 2).reshape(rows, SSM_WIDTH)
        z = zmix(hc, ys, conv_w_pw, ssm_w_glu, i, proj, 1024, 512)
        xf, h = matmul_res_norm(z, w_out, i, xf, xa_norm_g[i], 512, BF16, "out_proj")
        q = matmul_wcast(h, xa_w_q, i, 1024, 1024, BF16, "q_proj")
        mn = rmsnorm_rows(memf, mem_norm_g[i], 256, BF16)
        kv = matmul_wcast(mn, xa_w_kv, i, 512, 1024, BF16, "kv_proj")
        o = attention(q, kv, bsz, seq, 1024)
        xf, h = matmul_res_norm(o, xa_w_o, i, xf, ffn_norm_g[i], 512, BF16, "o_proj")
        act = ffn_up(h, ffn_w_up, ffn_dw_w, i, bsz, seq, 1024, 512)
        if i + 1 < depth:
            xf, h = matmul_cols_res_norm(act, w_down_b, i, xf, mix_norm_g[i + 1],
                                         1024, 512, 2, BF16, True, "down_proj")
        else:
            out = matmul_cols_res_norm(act, w_down_b, i, xf, final_norm_g,
                                       1024, 512, 2, F32, False, "down_proj_final")[0]
    return out.reshape(bsz, seq, d)
```

```python
import functools
import math

import jax
import jax.numpy as jnp
from jax import lax
from jax.experimental import pallas as pl
from jax.experimental.pallas import tpu as pltpu

D_MODEL = 2048
MEM_LEN = 256
CONV_WIDTH = 1024
CONV_K = 31
SSM_WIDTH = 1024
SSM_GROUP = 16
SSM_GROUPS = SSM_WIDTH // SSM_GROUP
SSM_STATE = 64
XA_HEADS = 4
XA_HEAD_DIM = D_MODEL // XA_HEADS
D_FF = 5632
FFN_K = 3
EPS = 1e-6

LANES = 128
SUBLANES = 8
MXU_COLS = 256
VMEM_LIMIT = 56 * 1024 * 1024

BF16 = jnp.bfloat16
F32 = jnp.float32


def _params(*sem):
    return pltpu.CompilerParams(dimension_semantics=sem, vmem_limit_bytes=VMEM_LIMIT)


def _rms(x, g):
    return x * lax.rsqrt(jnp.mean(x * x, axis=-1, keepdims=True) + EPS) * g


def _sigmoid(x):
    return 0.5 * jnp.tanh(0.5 * x) + 0.5


def _silu(x):
    hx = 0.5 * x
    return hx * jnp.tanh(hx) + hx


def _rmsnorm_kernel(x_ref, g_ref, o_ref):
    o_ref[...] = _rms(x_ref[...], g_ref[...]).astype(o_ref.dtype)


def rmsnorm_rows(x, g, tm, out_dtype):
    m, d = x.shape
    return pl.pallas_call(
        _rmsnorm_kernel,
        grid=(m // tm,),
        in_specs=[pl.BlockSpec((tm, d), lambda i: (i, 0)),
                  pl.BlockSpec((1, d), lambda i: (0, 0))],
        out_specs=pl.BlockSpec((tm, d), lambda i: (i, 0)),
        out_shape=jax.ShapeDtypeStruct((m, d), out_dtype),
        compiler_params=_params("parallel"),
        name="rmsnorm",
    )(x, g.reshape(1, d))


def _mm_kernel(a_ref, w_ref, o_ref):
    o_ref[...] = jnp.dot(a_ref[...], w_ref[...], preferred_element_type=F32).astype(o_ref.dtype)


def matmul(a, w, li, tm, tn, out_dtype, name):
    m, k = a.shape
    n = w.shape[2]
    return pl.pallas_call(
        _mm_kernel,
        grid=(m // tm, n // tn),
        in_specs=[pl.BlockSpec((tm, k), lambda i, j: (i, 0)),
                  pl.BlockSpec((None, k, tn), lambda i, j: (li, 0, j))],
        out_specs=pl.BlockSpec((tm, tn), lambda i, j: (i, j)),
        out_shape=jax.ShapeDtypeStruct((m, n), out_dtype),
        compiler_params=_params("parallel", "arbitrary"),
        name=name,
    )(a, w)


def _mm_wcast_kernel(a_ref, w_ref, o_ref, wb_ref):
    @pl.when(pl.program_id(1) == 0)
    def _():
        wb_ref[...] = w_ref[...].astype(BF16)

    o_ref[...] = jnp.dot(a_ref[...], wb_ref[...], preferred_element_type=F32).astype(o_ref.dtype)


def matmul_wcast(a, w, li, tm, tn, out_dtype, name):
    m, k = a.shape
    n = w.shape[2]
    return pl.pallas_call(
        _mm_wcast_kernel,
        grid=(n // tn, m // tm),
        in_specs=[pl.BlockSpec((tm, k), lambda j, i: (i, 0)),
                  pl.BlockSpec((None, k, tn), lambda j, i: (li, 0, j))],
        out_specs=pl.BlockSpec((tm, tn), lambda j, i: (i, j)),
        out_shape=jax.ShapeDtypeStruct((m, n), out_dtype),
        scratch_shapes=[pltpu.VMEM((k, tn), BF16)],
        compiler_params=_params("parallel", "arbitrary"),
        name=name,
    )(a, w)


def _mm_res_norm_kernel(a_ref, w_ref, x_ref, g_ref, xo_ref, h_ref, wb_ref):
    @pl.when(pl.program_id(0) == 0)
    def _():
        wb_ref[...] = w_ref[...].astype(BF16)

    xn = x_ref[...] + jnp.dot(a_ref[...], wb_ref[...], preferred_element_type=F32)
    xo_ref[...] = xn
    h_ref[...] = _rms(xn, g_ref[...]).astype(h_ref.dtype)


def matmul_res_norm(a, w, li, x, g, tm, h_dtype, name):
    m, kdim = a.shape
    n = w.shape[2]
    row = lambda i: (i, 0)
    return pl.pallas_call(
        _mm_res_norm_kernel,
        grid=(m // tm,),
        in_specs=[pl.BlockSpec((tm, kdim), row),
                  pl.BlockSpec((None, kdim, n), lambda i: (li, 0, 0), pipeline_mode=pl.Buffered(1)),
                  pl.BlockSpec((tm, n), row),
                  pl.BlockSpec((1, n), lambda i: (0, 0))],
        out_specs=[pl.BlockSpec((tm, n), row), pl.BlockSpec((tm, n), row)],
        out_shape=[jax.ShapeDtypeStruct((m, n), F32), jax.ShapeDtypeStruct((m, n), h_dtype)],
        scratch_shapes=[pltpu.VMEM((kdim, n), BF16)],
        compiler_params=_params("arbitrary"),
        name=name,
    )(a, w, x, g.reshape(1, n))


def _mm_cols_res_norm_kernel(a_ref, w_ref, x_ref, g_ref, *refs, nk, nj, tn, emit_x):
    if emit_x:
        xo_ref, h_ref, row_ref = refs
    else:
        h_ref, row_ref = refs
    k = pl.program_id(1)
    j = pl.program_id(2)

    def chunk(first, last):
        base = x_ref[...] if first else row_ref[j]
        val = base + jnp.dot(a_ref[...], w_ref[...], preferred_element_type=F32)
        row_ref[j] = val
        if last and emit_x:
            xo_ref[...] = val

    pl.when(k == 0)(functools.partial(chunk, True, nk == 1))
    if nk > 2:
        pl.when((k > 0) & (k < nk - 1))(functools.partial(chunk, False, False))
    if nk > 1:
        pl.when(k == nk - 1)(functools.partial(chunk, False, True))

    @pl.when((k == nk - 1) & (j == nj - 1))
    def _():
        ssq = jnp.zeros((row_ref.shape[1], 1), F32)
        for jj in range(nj):
            blk = row_ref[jj]
            ssq = ssq + jnp.sum(blk * blk, axis=-1, keepdims=True)
        inv = lax.rsqrt(ssq * (1.0 / (nj * tn)) + EPS)
        for jj in range(nj):
            sl = slice(jj * tn, (jj + 1) * tn)
            h_ref[:, sl] = (row_ref[jj] * inv * g_ref[:, sl]).astype(h_ref.dtype)


def matmul_cols_res_norm(a, w, li, x, g, tm, tn, nk, h_dtype, emit_x, name):
    m, kdim = a.shape
    n = w.shape[2]
    nj = n // tn
    tk = kdim // nk
    last = nk - 1
    out_specs = [pl.BlockSpec((tm, n), lambda i, k, j: (i, 0))]
    out_shape = [jax.ShapeDtypeStruct((m, n), h_dtype)]
    if emit_x:
        out_specs = [pl.BlockSpec((tm, tn), lambda i, k, j: (i, jnp.where(k == last, j, 0)))] + out_specs
        out_shape = [jax.ShapeDtypeStruct((m, n), F32)] + out_shape
    return pl.pallas_call(
        functools.partial(_mm_cols_res_norm_kernel, nk=nk, nj=nj, tn=tn, emit_x=emit_x),
        grid=(m // tm, nk, nj),
        in_specs=[pl.BlockSpec((tm, tk), lambda i, k, j: (i, k)),
                  pl.BlockSpec((None, tk, tn), lambda i, k, j: (li, k, j)),
                  pl.BlockSpec((tm, tn), lambda i, k, j: (i, jnp.where(k == 0, j, nj - 1))),
                  pl.BlockSpec((1, n), lambda i, k, j: (0, 0))],
        out_specs=out_specs,
        out_shape=out_shape,
        scratch_shapes=[pltpu.VMEM((nj, tm, tn), F32)],
        compiler_params=_params("parallel", "arbitrary", "arbitrary"),
        name=name,
    )(a, w, x, g.reshape(1, n))


CONV_HALO = 32
CONV_RB = 32


def _conv_kernel(a_ref, b_ref, w_ref, bias_ref, lng_ref, lnb_ref, o_ref, ext_ref, y_ref, *, tm):
    nslab = CONV_WIDTH // LANES
    t = pl.program_id(1)

    @pl.when(t == 0)
    def _():
        ext_ref[:, 0:CONV_HALO, :] = jnp.zeros((nslab, CONV_HALO, LANES), F32)

    for c in range(nslab):
        sl = slice(c * LANES, (c + 1) * LANES)
        ext_ref[c, CONV_HALO:CONV_HALO + tm, :] = a_ref[:, sl] * _sigmoid(b_ref[:, sl])

    first = CONV_HALO - (CONV_K - 1)
    for c in range(nslab):
        sl = slice(c * LANES, (c + 1) * LANES)

        def row_block(rb, carry, c=c, sl=sl):
            r0 = pl.multiple_of(rb * CONV_RB, CONV_RB)
            acc = [jnp.broadcast_to(bias_ref[:, sl], (CONV_RB, LANES)), None]
            for k in range(CONV_K):
                term = w_ref[k:k + 1, sl] * ext_ref[c, pl.ds(r0 + first + k, CONV_RB), :]
                acc[k % 2] = term if acc[k % 2] is None else acc[k % 2] + term
            y_ref[pl.ds(r0, CONV_RB), sl] = acc[0] + acc[1]
            return carry

        lax.fori_loop(0, tm // CONV_RB, row_block, 0, unroll=8)

    ext_ref[:, 0:CONV_HALO, :] = ext_ref[:, tm:tm + CONV_HALO, :]

    def ln_block(rb, carry):
        r0 = pl.multiple_of(rb * CONV_RB, CONV_RB)
        y = y_ref[pl.ds(r0, CONV_RB), :]
        mu = jnp.mean(y, axis=-1, keepdims=True)
        yc = y - mu
        yn = yc * lax.rsqrt(jnp.mean(yc * yc, axis=-1, keepdims=True) + EPS)
        yn = yn * lng_ref[...] + lnb_ref[...]
        o_ref[pl.ds(r0, CONV_RB), :] = _silu(yn).astype(o_ref.dtype)
        return carry

    lax.fori_loop(0, tm // CONV_RB, ln_block, 0, unroll=16)


def conv_branch(proj, dw_w, dw_b, ln_g, ln_b, bsz, seq, tm):
    nt = seq // tm
    cw = CONV_WIDTH
    row = lambda b, t: (b * nt + t, 0)
    vec = lambda b, t: (0, 0)
    return pl.pallas_call(
        functools.partial(_conv_kernel, tm=tm),
        grid=(bsz, nt),
        in_specs=[pl.BlockSpec((tm, cw), lambda b, t: (b * nt + t, 0)),
                  pl.BlockSpec((tm, cw), lambda b, t: (b * nt + t, 1)),
                  pl.BlockSpec((CONV_K, cw), vec),
                  pl.BlockSpec((1, cw), vec),
                  pl.BlockSpec((1, cw), vec),
                  pl.BlockSpec((1, cw), vec)],
        out_specs=pl.BlockSpec((tm, cw), row),
        out_shape=jax.ShapeDtypeStruct((bsz * seq, cw), BF16),
        scratch_shapes=[pltpu.VMEM((cw // LANES, CONV_HALO + tm, LANES), F32),
                        pltpu.VMEM((tm, cw), F32)],
        compiler_params=_params("parallel", "arbitrary"),
        name="conv_branch",
    )(proj, proj, dw_w, dw_b.reshape(1, cw), ln_g.reshape(1, cw), ln_b.reshape(1, cw))


SSM_GB = 16
SSM_UB = SSM_GB * SSM_GROUP
SSM_CB = SSM_GB * SSM_STATE
SSM_NJ = 2 * SSM_CB // LANES
SSM_PAD = 4


def _zoh(ar, ai, ldt):
    ar = jnp.minimum(ar, -1e-4)
    dt = jnp.exp(ldt)
    mag = jnp.exp(dt * ar)
    return mag * jnp.cos(dt * ai), mag * jnp.sin(dt * ai), ar


def _ssm_prep_kernel(arow_ref, avec_ref, bt_ref, ct_ref, abar_ref, bw_ref, cw_ref):
    are, aim, _ = _zoh(avec_ref[0], avec_ref[1], avec_ref[2])
    abar_ref[0] = are
    abar_ref[1] = aim

    abar_re, abar_im, ar = _zoh(arow_ref[0], arow_ref[1], arow_ref[2])
    ai = arow_ref[1]
    den = ar * ar + ai * ai
    nr = abar_re - 1.0
    z_re = (nr * ar + abar_im * ai) / den
    z_im = (abar_im * ar - nr * ai) / den
    br = bt_ref[0]
    bi = bt_ref[1]
    bbar = (z_re * br - z_im * bi, z_re * bi + z_im * br)

    rg = lax.broadcasted_iota(jnp.int32, (SSM_UB, SSM_CB), 0) // SSM_GROUP
    cg = lax.broadcasted_iota(jnp.int32, (SSM_UB, SSM_CB), 1) // SSM_STATE
    for part in range(2):
        tiled = jnp.concatenate([bbar[part]] * SSM_GB, axis=0)
        bw_ref[:, part * SSM_CB:(part + 1) * SSM_CB] = jnp.where(rg == cg, tiled, 0.0).astype(BF16)

    rg = lax.broadcasted_iota(jnp.int32, (SSM_CB, SSM_UB), 0) // SSM_STATE
    cg = lax.broadcasted_iota(jnp.int32, (SSM_CB, SSM_UB), 1) // SSM_GROUP
    for part, sign in ((0, 1.0), (1, -1.0)):
        tiled = jnp.concatenate([sign * ct_ref[part]] * SSM_GB, axis=0)
        cw_ref[part * SSM_CB:(part + 1) * SSM_CB, :] = jnp.where(rg == cg, tiled, 0.0).astype(BF16)


def ssm_prep(a_re, a_im, log_dt, b_re, b_im, c_re, c_im):
    nl = a_re.shape[0]
    ngb = SSM_GROUPS // SSM_GB
    nblk = nl * ngb
    ldt = jnp.broadcast_to(log_dt[:, :, None], a_re.shape)
    a3 = jnp.stack([a_re, a_im, ldt], axis=1).reshape(nl, 3, ngb, SSM_CB)
    a3 = jnp.transpose(a3, (0, 2, 1, 3)).reshape(nblk, 3, SSM_CB)
    arow = a3.reshape(nblk, 3, 1, SSM_CB)
    avec = a3.reshape(nblk, 3, SUBLANES, LANES)

    def b_t(b):
        v = b.reshape(nl, ngb, SSM_CB, SSM_GROUP)
        return jnp.transpose(v, (0, 1, 3, 2)).reshape(nblk, SSM_GROUP, SSM_CB)

    def c_t(c):
        v = c.reshape(nl, ngb, SSM_GB, SSM_GROUP, SSM_STATE)
        return jnp.transpose(v, (0, 1, 4, 2, 3)).reshape(nblk, SSM_STATE, SSM_UB)

    bt = jnp.stack([b_t(b_re), b_t(b_im)], axis=1)
    ct = jnp.stack([c_t(c_re), c_t(c_im)], axis=1)
    blk = lambda i: (i, 0, 0, 0)
    return pl.pallas_call(
        _ssm_prep_kernel,
        grid=(nblk,),
        in_specs=[pl.BlockSpec((None, 3, 1, SSM_CB), blk),
                  pl.BlockSpec((None, 3, SUBLANES, LANES), blk),
                  pl.BlockSpec((None, 2, SSM_GROUP, SSM_CB), blk),
                  pl.BlockSpec((None, 2, SSM_STATE, SSM_UB), blk)],
        out_specs=[pl.BlockSpec((None, 2, SUBLANES, LANES), blk),
                   pl.BlockSpec((None, SSM_UB, 2 * SSM_CB), lambda i: (i, 0, 0)),
                   pl.BlockSpec((None, 2 * SSM_CB, SSM_UB), lambda i: (i, 0, 0))],
        out_shape=[jax.ShapeDtypeStruct((nblk, 2, SUBLANES, LANES), F32),
                   jax.ShapeDtypeStruct((nblk, SSM_UB, 2 * SSM_CB), BF16),
                   jax.ShapeDtypeStruct((nblk, 2 * SSM_CB, SSM_UB), BF16)],
        compiler_params=_params("parallel"),
        name="ssm_prep",
    )(arow, avec, bt, ct)


def _ssm_kernel(u_ref, bw_ref, cw_ref, abar_ref, d_ref, o_ref, s_ref, st_ref, *, tt, nb, ng):
    pitch = tt + SSM_PAD
    half = SSM_NJ // 2
    chains = [(b, gl) for b in range(nb) for gl in range(ng)]
    t_idx = pl.program_id(1)

    @pl.when(t_idx == 0)
    def _():
        st_ref[...] = jnp.zeros(st_ref.shape, F32)

    for q, (b, gl) in enumerate(chains):
        ucols = slice(gl * SSM_UB, (gl + 1) * SSM_UB)
        bu = jnp.dot(u_ref[b, :, ucols].astype(BF16), bw_ref[gl], preferred_element_type=F32)
        for j in range(SSM_NJ):
            s_ref[q, j * pitch:j * pitch + tt, :] = bu[:, j * LANES:(j + 1) * LANES]

    ar = [abar_ref[gl, 0] for gl in range(ng)]
    ai = [abar_ref[gl, 1] for gl in range(ng)]

    for gl in range(ng):
        qs = [b * ng + gl for b in range(nb)]
        state = [(st_ref[2 * q], st_ref[2 * q + 1]) for q in qs]
        for t in range(tt):
            for idx, q in enumerate(qs):
                xr, xi = state[idx]
                sq = s_ref.at[q]
                br = sq[pl.ds(t, half, stride=pitch), :]
                bi = sq[pl.ds(half * pitch + t, half, stride=pitch), :]
                nr = ar[gl] * xr - ai[gl] * xi + br
                ni = ar[gl] * xi + ai[gl] * xr + bi
                sq[pl.ds(t, half, stride=pitch), :] = nr
                sq[pl.ds(half * pitch + t, half, stride=pitch), :] = ni
                state[idx] = (nr, ni)
        for idx, q in enumerate(qs):
            st_ref[2 * q] = state[idx][0]
            st_ref[2 * q + 1] = state[idx][1]

    for q, (b, gl) in enumerate(chains):
        ucols = slice(gl * SSM_UB, (gl + 1) * SSM_UB)
        xs = jnp.concatenate([s_ref[q, j * pitch:j * pitch + tt, :].astype(BF16) for j in range(SSM_NJ)], axis=1)
        acc = jnp.dot(xs, cw_ref[gl], preferred_element_type=F32)
        y = acc + d_ref[:, ucols] * u_ref[b, :, ucols]
        o_ref[b, :, ucols] = jax.nn.gelu(y).astype(o_ref.dtype)


def ssm_branch(proj3, bw, cw, abar, d_skip, li, tt, ng):
    bsz, seq, _ = proj3.shape
    ngb = SSM_WIDTH // SSM_UB
    uw = ng * SSM_UB
    col0 = 2 * CONV_WIDTH // uw
    blk0 = li * ngb // ng
    pitch = tt + SSM_PAD
    return pl.pallas_call(
        functools.partial(_ssm_kernel, tt=tt, nb=bsz, ng=ng),
        grid=(ngb // ng, seq // tt),
        in_specs=[pl.BlockSpec((bsz, tt, uw), lambda g, t: (0, t, col0 + g)),
                  pl.BlockSpec((ng, SSM_UB, 2 * SSM_CB), lambda g, t: (blk0 + g, 0, 0)),
                  pl.BlockSpec((ng, 2 * SSM_CB, SSM_UB), lambda g, t: (blk0 + g, 0, 0)),
                  pl.BlockSpec((ng, 2, SUBLANES, LANES), lambda g, t: (blk0 + g, 0, 0, 0)),
                  pl.BlockSpec((None, 1, uw), lambda g, t: (li, 0, g))],
        out_specs=pl.BlockSpec((bsz, tt, uw), lambda g, t: (0, t, g)),
        out_shape=jax.ShapeDtypeStruct((bsz, seq, SSM_WIDTH), BF16),
        scratch_shapes=[pltpu.VMEM((bsz * ng, SSM_NJ * pitch, LANES), F32),
                        pltpu.VMEM((2 * bsz * ng, SUBLANES, LANES), F32)],
        compiler_params=_params("parallel", "arbitrary"),
        name="ssm_branch",
    )(proj3, bw, cw, abar, d_skip)


def _zmix_kernel(hc_ref, ys_ref, wpw_ref, wga_ref, wgb_ref, la_ref, lb_ref, z_ref, pwb_ref, gab_ref, gbb_ref):
    @pl.when(pl.program_id(1) == 0)
    def _():
        pwb_ref[...] = wpw_ref[...].astype(BF16)
        gab_ref[...] = wga_ref[...].astype(BF16)
        gbb_ref[...] = wgb_ref[...].astype(BF16)

    hc = hc_ref[...]
    ys = ys_ref[...]
    for c in range(z_ref.shape[1] // MXU_COLS):
        sl = slice(c * MXU_COLS, (c + 1) * MXU_COLS)
        ya = jnp.dot(hc, pwb_ref[:, sl], preferred_element_type=F32)
        ga = jnp.dot(ys, gab_ref[:, sl], preferred_element_type=F32)
        gb = jnp.dot(ys, gbb_ref[:, sl], preferred_element_type=F32)
        yb = ga * _sigmoid(gb)
        z = _sigmoid(la_ref[:, sl]) * ya + _sigmoid(lb_ref[:, sl]) * yb
        z_ref[:, sl] = z.astype(z_ref.dtype)


def zmix(hc, ys, w_pw, w_glu, li, proj, tm, tn):
    m = hc.shape[0]
    nj = D_MODEL // tn
    gate0 = (2 * CONV_WIDTH + SSM_WIDTH) // tn
    return pl.pallas_call(
        _zmix_kernel,
        grid=(nj, m // tm),
        in_specs=[pl.BlockSpec((tm, CONV_WIDTH), lambda j, i: (i, 0)),
                  pl.BlockSpec((tm, SSM_WIDTH), lambda j, i: (i, 0)),
                  pl.BlockSpec((None, CONV_WIDTH, tn), lambda j, i: (li, 0, j)),
                  pl.BlockSpec((None, SSM_WIDTH, tn), lambda j, i: (li, 0, j)),
                  pl.BlockSpec((None, SSM_WIDTH, tn), lambda j, i: (li, 0, nj + j)),
                  pl.BlockSpec((tm, tn), lambda j, i: (i, gate0 + j)),
                  pl.BlockSpec((tm, tn), lambda j, i: (i, gate0 + nj + j))],
        out_specs=pl.BlockSpec((tm, tn), lambda j, i: (i, j)),
        out_shape=jax.ShapeDtypeStruct((m, D_MODEL), BF16),
        scratch_shapes=[pltpu.VMEM((CONV_WIDTH, tn), BF16),
                        pltpu.VMEM((SSM_WIDTH, tn), BF16),
                        pltpu.VMEM((SSM_WIDTH, tn), BF16)],
        compiler_params=_params("parallel", "arbitrary"),
        name="zmix",
    )(hc, ys, w_pw, w_glu, w_glu, proj, proj)


def _attn_kernel(q_ref, k_ref, v_ref, o_ref):
    scale = XA_HEAD_DIM ** -0.5
    for h in range(XA_HEADS):
        sl = slice(h * XA_HEAD_DIM, (h + 1) * XA_HEAD_DIM)
        s = lax.dot_general(q_ref[:, sl], k_ref[:, sl], (((1,), (1,)), ((), ())),
                            preferred_element_type=F32) * scale
        s = s - jnp.max(s, axis=-1, keepdims=True)
        p = jnp.exp(s)
        p = p / jnp.sum(p, axis=-1, keepdims=True)
        o = jnp.dot(p.astype(BF16), v_ref[:, sl], preferred_element_type=F32)
        o_ref[:, sl] = o.astype(o_ref.dtype)


def attention(q, kv, bsz, seq, tm):
    nt = seq // tm
    return pl.pallas_call(
        _attn_kernel,
        grid=(bsz, nt),
        in_specs=[pl.BlockSpec((tm, D_MODEL), lambda b, t: (b * nt + t, 0)),
                  pl.BlockSpec((MEM_LEN, D_MODEL), lambda b, t: (b, 0)),
                  pl.BlockSpec((MEM_LEN, D_MODEL), lambda b, t: (b, 1))],
        out_specs=pl.BlockSpec((tm, D_MODEL), lambda b, t: (b * nt + t, 0)),
        out_shape=jax.ShapeDtypeStruct((bsz * seq, D_MODEL), BF16),
        compiler_params=_params("parallel", "arbitrary"),
        name="attention",
    )(q, kv, kv)


FFN_HALO = 8
FFN_RB = 32


def _ffn_up_kernel(h_ref, wg_ref, wv_ref, dg_ref, dv_ref, o_ref, wgb_ref, wvb_ref,
                   dgh_ref, eg_ref, ev_ref, *, tm, tn, nt):
    nslab = tn // LANES
    s = pl.program_id(1)

    @pl.when(s == 0)
    def _():
        wgb_ref[...] = wg_ref[...].astype(BF16)
        wvb_ref[...] = wv_ref[...].astype(BF16)
        dgh_ref[...] = 0.5 * dg_ref[...]

    @pl.when(s % nt == 0)
    def _():
        eg_ref[:, 0:FFN_HALO, :] = jnp.zeros((nslab, FFN_HALO, LANES), F32)
        ev_ref[:, 0:FFN_HALO, :] = jnp.zeros((nslab, FFN_HALO, LANES), F32)

    h = h_ref[...]
    for e_ref, wb_ref in ((eg_ref, wgb_ref), (ev_ref, wvb_ref)):
        for c2 in range(tn // MXU_COLS):
            u = jnp.dot(h, wb_ref[:, c2 * MXU_COLS:(c2 + 1) * MXU_COLS], preferred_element_type=F32)
            for half in range(MXU_COLS // LANES):
                c = c2 * (MXU_COLS // LANES) + half
                e_ref[c, FFN_HALO:FFN_HALO + tm, :] = u[:, half * LANES:(half + 1) * LANES]

    first = FFN_HALO - (FFN_K - 1)
    for c in range(nslab):
        sl = slice(c * LANES, (c + 1) * LANES)
        for rb in range(tm // FFN_RB):
            r0 = rb * FFN_RB
            hg = dgh_ref[0:1, sl] * eg_ref[c, first + r0:first + r0 + FFN_RB, :]
            v = dv_ref[0:1, sl] * ev_ref[c, first + r0:first + r0 + FFN_RB, :]
            for k in range(1, FFN_K):
                hg = hg + dgh_ref[k:k + 1, sl] * eg_ref[c, first + r0 + k:first + r0 + k + FFN_RB, :]
                v = v + dv_ref[k:k + 1, sl] * ev_ref[c, first + r0 + k:first + r0 + k + FFN_RB, :]
            o_ref[r0:r0 + FFN_RB, sl] = ((hg * jnp.tanh(hg) + hg) * v).astype(o_ref.dtype)

    eg_ref[:, 0:FFN_HALO, :] = eg_ref[:, tm:tm + FFN_HALO, :]
    ev_ref[:, 0:FFN_HALO, :] = ev_ref[:, tm:tm + FFN_HALO, :]


def ffn_up(h, w_up, dw_w, li, bsz, seq, tm, tn):
    nt = seq // tm
    nj = D_FF // tn
    slabs = (tn // LANES, FFN_HALO + tm, LANES)
    return pl.pallas_call(
        functools.partial(_ffn_up_kernel, tm=tm, tn=tn, nt=nt),
        grid=(nj, bsz * nt),
        in_specs=[pl.BlockSpec((tm, D_MODEL), lambda j, s: (s, 0)),
                  pl.BlockSpec((None, D_MODEL, tn), lambda j, s: (li, 0, j)),
                  pl.BlockSpec((None, D_MODEL, tn), lambda j, s: (li, 0, nj + j)),
                  pl.BlockSpec((None, FFN_K, tn), lambda j, s: (li, 0, j)),
                  pl.BlockSpec((None, FFN_K, tn), lambda j, s: (li, 0, nj + j))],
        out_specs=pl.BlockSpec((tm, tn), lambda j, s: (s, j)),
        out_shape=jax.ShapeDtypeStruct((bsz * seq, D_FF), BF16),
        scratch_shapes=[pltpu.VMEM((D_MODEL, tn), BF16),
                        pltpu.VMEM((D_MODEL, tn), BF16),
                        pltpu.VMEM((FFN_K, tn), F32),
                        pltpu.VMEM(slabs, F32),
                        pltpu.VMEM(slabs, F32)],
        compiler_params=_params("parallel", "arbitrary"),
        name="ffn_up",
    )(h, w_up, w_up, dw_w, dw_w)


def kernel(x, mem, mix_norm_g, w_in, conv_dw_w, conv_dw_b, conv_ln_g, conv_ln_b, conv_w_pw,
           ssm_a_re, ssm_a_im, ssm_log_dt, ssm_b_re, ssm_b_im, ssm_c_re, ssm_c_im, ssm_d,
           ssm_w_glu, w_out, xa_norm_g, mem_norm_g, xa_w_q, xa_w_kv, xa_w_o,
           ffn_norm_g, ffn_w_up, ffn_dw_w, ffn_w_down, final_norm_g):
    bsz, seq, d = x.shape
    depth = w_in.shape[0]
    rows = bsz * seq
    xf = x.reshape(rows, d)
    memf = mem.reshape(bsz * MEM_LEN, d)

    w_down_b = ffn_w_down.astype(BF16)
    abar, bw, cw = ssm_prep(ssm_a_re, ssm_a_im, ssm_log_dt, ssm_b_re, ssm_b_im, ssm_c_re, ssm_c_im)
    d_skip = ssm_d.reshape(depth, 1, SSM_WIDTH)

    h = rmsnorm_rows(xf, mix_norm_g[0], 512, BF16)
    out = None
    for i in range(depth):
        proj = matmul_wcast(h, w_in, i, 1024, 1024, F32, "in_proj")
        hc = conv_branch(proj, conv_dw_w[i], conv_dw_b[i], conv_ln_g[i], conv_ln_b[i], bsz, seq, 512)
        ys = ssm_branch(proj.reshape(bsz, seq, -1), bw, cw, abar, d_skip, i, 512, 2).reshape(rows, SSM_WIDTH)
        z = zmix(hc, ys, conv_w_pw, ssm_w_glu, i, proj, 1024, 512)
        xf, h = matmul_res_norm(z, w_out, i, xf, xa_norm_g[i], 512, BF16, "out_proj")
        q = matmul_wcast(h, xa_w_q, i, 1024, 1024, BF16, "q_proj")
        mn = rmsnorm_rows(memf, mem_norm_g[i], 256, BF16)
        kv = matmul_wcast(mn, xa_w_kv, i, 512, 1024, BF16, "kv_proj")
        o = attention(q, kv, bsz, seq, 1024)
        xf, h = matmul_res_norm(o, xa_w_o, i, xf, ffn_norm_g[i], 512, BF16, "o_proj")
        act = ffn_up(h, ffn_w_up, ffn_dw_w, i, bsz, seq, 1024, 512)
        if i + 1 < depth:
            xf, h = matmul_cols_res_norm(act, w_down_b, i, xf, mix_norm_g[i + 1],
                                         1024, 512, 2, BF16, True, "down_proj")
        else:
            out = matmul_cols_res_norm(act, w_down_b, i, xf, final_norm_g,
                                       1024, 512, 2, F32, False, "down_proj_final")[0]
    return out.reshape(bsz, seq, d)
```

```python
import functools
import math

import jax
import jax.numpy as jnp
from jax import lax
from jax.experimental import pallas as pl
from jax.experimental.pallas import tpu as pltpu

D_MODEL = 2048
MEM_LEN = 256
CONV_WIDTH = 1024
CONV_K = 31
SSM_WIDTH = 1024
SSM_GROUP = 16
SSM_GROUPS = SSM_WIDTH // SSM_GROUP
SSM_STATE = 64
XA_HEADS = 4
XA_HEAD_DIM = D_MODEL // XA_HEADS
D_FF = 5632
FFN_K = 3
EPS = 1e-6

LANES = 128
SUBLANES = 8
MXU_COLS = 256
VMEM_LIMIT = 56 * 1024 * 1024

BF16 = jnp.bfloat16
F32 = jnp.float32


def _params(*sem):
    return pltpu.CompilerParams(dimension_semantics=sem, vmem_limit_bytes=VMEM_LIMIT)


def _rms(x, g):
    return x * lax.rsqrt(jnp.mean(x * x, axis=-1, keepdims=True) + EPS) * g


def _sigmoid(x):
    return 0.5 * jnp.tanh(0.5 * x) + 0.5


def _silu(x):
    hx = 0.5 * x
    return hx * jnp.tanh(hx) + hx


def _rmsnorm_kernel(x_ref, g_ref, o_ref):
    o_ref[...] = _rms(x_ref[...], g_ref[...]).astype(o_ref.dtype)


def rmsnorm_rows(x, g, tm, out_dtype):
    m, d = x.shape
    return pl.pallas_call(
        _rmsnorm_kernel,
        grid=(m // tm,),
        in_specs=[pl.BlockSpec((tm, d), lambda i: (i, 0)),
                  pl.BlockSpec((1, d), lambda i: (0, 0))],
        out_specs=pl.BlockSpec((tm, d), lambda i: (i, 0)),
        out_shape=jax.ShapeDtypeStruct((m, d), out_dtype),
        compiler_params=_params("parallel"),
        name="rmsnorm",
    )(x, g.reshape(1, d))


def _mm_kernel(a_ref, w_ref, o_ref):
    o_ref[...] = jnp.dot(a_ref[...], w_ref[...], preferred_element_type=F32).astype(o_ref.dtype)


def matmul(a, w, li, tm, tn, out_dtype, name):
    m, k = a.shape
    n = w.shape[2]
    return pl.pallas_call(
        _mm_kernel,
        grid=(m // tm, n // tn),
        in_specs=[pl.BlockSpec((tm, k), lambda i, j: (i, 0)),
                  pl.BlockSpec((None, k, tn), lambda i, j: (li, 0, j))],
        out_specs=pl.BlockSpec((tm, tn), lambda i, j: (i, j)),
        out_shape=jax.ShapeDtypeStruct((m, n), out_dtype),
        compiler_params=_params("parallel", "arbitrary"),
        name=name,
    )(a, w)


def _mm_wcast_kernel(a_ref, w_ref, o_ref, wb_ref):
    @pl.when(pl.program_id(1) == 0)
    def _():
        wb_ref[...] = w_ref[...].astype(BF16)

    o_ref[...] = jnp.dot(a_ref[...], wb_ref[...], preferred_element_type=F32).astype(o_ref.dtype)


def matmul_wcast(a, w, li, tm, tn, out_dtype, name):
    m, k = a.shape
    n = w.shape[2]
    return pl.pallas_call(
        _mm_wcast_kernel,
        grid=(n // tn, m // tm),
        in_specs=[pl.BlockSpec((tm, k), lambda j, i: (i, 0)),
                  pl.BlockSpec((None, k, tn), lambda j, i: (li, 0, j))],
        out_specs=pl.BlockSpec((tm, tn), lambda j, i: (i, j)),
        out_shape=jax.ShapeDtypeStruct((m, n), out_dtype),
        scratch_shapes=[pltpu.VMEM((k, tn), BF16)],
        compiler_params=_params("parallel", "arbitrary"),
        name=name,
    )(a, w)


def _mm_res_norm_kernel(a_ref, w_ref, x_ref, g_ref, xo_ref, h_ref, wb_ref):
    @pl.when(pl.program_id(0) == 0)
    def _():
        wb_ref[...] = w_ref[...].astype(BF16)

    xn = x_ref[...] + jnp.dot(a_ref[...], wb_ref[...], preferred_element_type=F32)
    xo_ref[...] = xn
    h_ref[...] = _rms(xn, g_ref[...]).astype(h_ref.dtype)


def matmul_res_norm(a, w, li, x, g, tm, h_dtype, name):
    m, kdim = a.shape
    n = w.shape[2]
    row = lambda i: (i, 0)
    return pl.pallas_call(
        _mm_res_norm_kernel,
        grid=(m // tm,),
        in_specs=[pl.BlockSpec((tm, kdim), row),
                  pl.BlockSpec((None, kdim, n), lambda i: (li, 0, 0), pipeline_mode=pl.Buffered(1)),
                  pl.BlockSpec((tm, n), row),
                  pl.BlockSpec((1, n), lambda i: (0, 0))],
        out_specs=[pl.BlockSpec((tm, n), row), pl.BlockSpec((tm, n), row)],
        out_shape=[jax.ShapeDtypeStruct((m, n), F32), jax.ShapeDtypeStruct((m, n), h_dtype)],
        scratch_shapes=[pltpu.VMEM((kdim, n), BF16)],
        compiler_params=_params("arbitrary"),
        name=name,
    )(a, w, x, g.reshape(1, n))


def _mm_cols_res_norm_kernel(a_ref, w_ref, x_ref, g_ref, *refs, nk, nj, tn, emit_x):
    if emit_x:
        xo_ref, h_ref, row_ref = refs
    else:
        h_ref, row_ref = refs
    k = pl.program_id(1)
    j = pl.program_id(2)

    def chunk(first, last):
        base = x_ref[...] if first else row_ref[j]
        val = base + jnp.dot(a_ref[...], w_ref[...], preferred_element_type=F32)
        row_ref[j] = val
        if last and emit_x:
            xo_ref[...] = val

    pl.when(k == 0)(functools.partial(chunk, True, nk == 1))
    if nk > 2:
        pl.when((k > 0) & (k < nk - 1))(functools.partial(chunk, False, False))
    if nk > 1:
        pl.when(k == nk - 1)(functools.partial(chunk, False, True))

    @pl.when((k == nk - 1) & (j == nj - 1))
    def _():
        ssq = jnp.zeros((row_ref.shape[1], 1), F32)
        for jj in range(nj):
            blk = row_ref[jj]
            ssq = ssq + jnp.sum(blk * blk, axis=-1, keepdims=True)
        inv = lax.rsqrt(ssq * (1.0 / (nj * tn)) + EPS)
        for jj in range(nj):
            sl = slice(jj * tn, (jj + 1) * tn)
            h_ref[:, sl] = (row_ref[jj] * inv * g_ref[:, sl]).astype(h_ref.dtype)


def matmul_cols_res_norm(a, w, li, x, g, tm, tn, nk, h_dtype, emit_x, name):
    m, kdim = a.shape
    n = w.shape[2]
    nj = n // tn
    tk = kdim // nk
    last = nk - 1
    out_specs = [pl.BlockSpec((tm, n), lambda i, k, j: (i, 0))]
    out_shape = [jax.ShapeDtypeStruct((m, n), h_dtype)]
    if emit_x:
        out_specs = [pl.BlockSpec((tm, tn), lambda i, k, j: (i, jnp.where(k == last, j, 0)))] + out_specs
        out_shape = [jax.ShapeDtypeStruct((m, n), F32)] + out_shape
    return pl.pallas_call(
        functools.partial(_mm_cols_res_norm_kernel, nk=nk, nj=nj, tn=tn, emit_x=emit_x),
        grid=(m // tm, nk, nj),
        in_specs=[pl.BlockSpec((tm, tk), lambda i, k, j: (i, k)),
                  pl.BlockSpec((None, tk, tn), lambda i, k, j: (li, k, j)),
                  pl.BlockSpec((tm, tn), lambda i, k, j: (i, jnp.where(k == 0, j, nj - 1))),
                  pl.BlockSpec((1, n), lambda i, k, j: (0, 0))],
        out_specs=out_specs,
        out_shape=out_shape,
        scratch_shapes=[pltpu.VMEM((nj, tm, tn), F32)],
        compiler_params=_params("parallel", "arbitrary", "arbitrary"),
        name=name,
    )(a, w, x, g.reshape(1, n))


CONV_HALO = 32
CONV_RB = 32


def _conv_kernel(a_ref, b_ref, w_ref, bias_ref, lng_ref, lnb_ref, o_ref, ext_ref, y_ref, *, tm):
    nslab = CONV_WIDTH // LANES
    t = pl.program_id(1)

    @pl.when(t == 0)
    def _():
        ext_ref[:, 0:CONV_HALO, :] = jnp.zeros((nslab, CONV_HALO, LANES), F32)

    for c in range(nslab):
        sl = slice(c * LANES, (c + 1) * LANES)
        ext_ref[c, CONV_HALO:CONV_HALO + tm, :] = a_ref[:, sl] * _sigmoid(b_ref[:, sl])

    first = CONV_HALO - (CONV_K - 1)
    for c in range(nslab):
        sl = slice(c * LANES, (c + 1) * LANES)

        def row_block(rb, carry, c=c, sl=sl):
            r0 = pl.multiple_of(rb * CONV_RB, CONV_RB)
            acc = [jnp.broadcast_to(bias_ref[:, sl], (CONV_RB, LANES)), None]
            for k in range(CONV_K):
                term = w_ref[k:k + 1, sl] * ext_ref[c, pl.ds(r0 + first + k, CONV_RB), :]
                acc[k % 2] = term if acc[k % 2] is None else acc[k % 2] + term
            y_ref[pl.ds(r0, CONV_RB), sl] = acc[0] + acc[1]
            return carry

        lax.fori_loop(0, tm // CONV_RB, row_block, 0, unroll=8)

    ext_ref[:, 0:CONV_HALO, :] = ext_ref[:, tm:tm + CONV_HALO, :]

    def ln_block(rb, carry):
        r0 = pl.multiple_of(rb * CONV_RB, CONV_RB)
        y = y_ref[pl.ds(r0, CONV_RB), :]
        mu = jnp.mean(y, axis=-1, keepdims=True)
        yc = y - mu
        yn = yc * lax.rsqrt(jnp.mean(yc * yc, axis=-1, keepdims=True) + EPS)
        yn = yn * lng_ref[...] + lnb_ref[...]
        o_ref[pl.ds(r0, CONV_RB), :] = _silu(yn).astype(o_ref.dtype)
        return carry

    lax.fori_loop(0, tm // CONV_RB, ln_block, 0, unroll=16)


def conv_branch(proj, dw_w, dw_b, ln_g, ln_b, bsz, seq, tm):
    nt = seq // tm
    cw = CONV_WIDTH
    row = lambda b, t: (b * nt + t, 0)
    vec = lambda b, t: (0, 0)
    return pl.pallas_call(
        functools.partial(_conv_kernel, tm=tm),
        grid=(bsz, nt),
        in_specs=[pl.BlockSpec((tm, cw), lambda b, t: (b * nt + t, 0)),
                  pl.BlockSpec((tm, cw), lambda b, t: (b * nt + t, 1)),
                  pl.BlockSpec((CONV_K, cw), vec),
                  pl.BlockSpec((1, cw), vec),
                  pl.BlockSpec((1, cw), vec),
                  pl.BlockSpec((1, cw), vec)],
        out_specs=pl.BlockSpec((tm, cw), row),
        out_shape=jax.ShapeDtypeStruct((bsz * seq, cw), BF16),
        scratch_shapes=[pltpu.VMEM((cw // LANES, CONV_HALO + tm, LANES), F32),
                        pltpu.VMEM((tm, cw), F32)],
        compiler_params=_params("parallel", "arbitrary"),
        name="conv_branch",
    )(proj, proj, dw_w, dw_b.reshape(1, cw), ln_g.reshape(1, cw), ln_b.reshape(1, cw))


SSM_GB = 16
SSM_UB = SSM_GB * SSM_GROUP
SSM_CB = SSM_GB * SSM_STATE
SSM_NJ = 2 * SSM_CB // LANES
SSM_PAD = 4


def _zoh(ar, ai, ldt):
    ar = jnp.minimum(ar, -1e-4)
    dt = jnp.exp(ldt)
    mag = jnp.exp(dt * ar)
    return mag * jnp.cos(dt * ai), mag * jnp.sin(dt * ai), ar


def _ssm_prep_kernel(arow_ref, avec_ref, bt_ref, ct_ref, abar_ref, bw_ref, cw_ref):
    are, aim, _ = _zoh(avec_ref[0], avec_ref[1], avec_ref[2])
    abar_ref[0] = are
    abar_ref[1] = aim

    abar_re, abar_im, ar = _zoh(arow_ref[0], arow_ref[1], arow_ref[2])
    ai = arow_ref[1]
    den = ar * ar + ai * ai
    nr = abar_re - 1.0
    z_re = (nr * ar + abar_im * ai) / den
    z_im = (abar_im * ar - nr * ai) / den
    br = bt_ref[0]
    bi = bt_ref[1]
    bbar = (z_re * br - z_im * bi, z_re * bi + z_im * br)

    rg = lax.broadcasted_iota(jnp.int32, (SSM_UB, SSM_CB), 0) // SSM_GROUP
    cg = lax.broadcasted_iota(jnp.int32, (SSM_UB, SSM_CB), 1) // SSM_STATE
    for part in range(2):
        tiled = jnp.concatenate([bbar[part]] * SSM_GB, axis=0)
        bw_ref[:, part * SSM_CB:(part + 1) * SSM_CB] = jnp.where(rg == cg, tiled, 0.0).astype(BF16)

    rg = lax.broadcasted_iota(jnp.int32, (SSM_CB, SSM_UB), 0) // SSM_STATE
    cg = lax.broadcasted_iota(jnp.int32, (SSM_CB, SSM_UB), 1) // SSM_GROUP
    for part, sign in ((0, 1.0), (1, -1.0)):
        tiled = jnp.concatenate([sign * ct_ref[part]] * SSM_GB, axis=0)
        cw_ref[part * SSM_CB:(part + 1) * SSM_CB, :] = jnp.where(rg == cg, tiled, 0.0).astype(BF16)


def ssm_prep(a_re, a_im, log_dt, b_re, b_im, c_re, c_im):
    nl = a_re.shape[0]
    ngb = SSM_GROUPS // SSM_GB
    nblk = nl * ngb
    ldt = jnp.broadcast_to(log_dt[:, :, None], a_re.shape)
    a3 = jnp.stack([a_re, a_im, ldt], axis=1).reshape(nl, 3, ngb, SSM_CB)
    a3 = jnp.transpose(a3, (0, 2, 1, 3)).reshape(nblk, 3, SSM_CB)
    arow = a3.reshape(nblk, 3, 1, SSM_CB)
    avec = a3.reshape(nblk, 3, SUBLANES, LANES)

    def b_t(b):
        v = b.reshape(nl, ngb, SSM_CB, SSM_GROUP)
        return jnp.transpose(v, (0, 1, 3, 2)).reshape(nblk, SSM_GROUP, SSM_CB)

    def c_t(c):
        v = c.reshape(nl, ngb, SSM_GB, SSM_GROUP, SSM_STATE)
        return jnp.transpose(v, (0, 1, 4, 2, 3)).reshape(nblk, SSM_STATE, SSM_UB)

    bt = jnp.stack([b_t(b_re), b_t(b_im)], axis=1)
    ct = jnp.stack([c_t(c_re), c_t(c_im)], axis=1)
    blk = lambda i: (i, 0, 0, 0)
    return pl.pallas_call(
        _ssm_prep_kernel,
        grid=(nblk,),
        in_specs=[pl.BlockSpec((None, 3, 1, SSM_CB), blk),
                  pl.BlockSpec((None, 3, SUBLANES, LANES), blk),
                  pl.BlockSpec((None, 2, SSM_GROUP, SSM_CB), blk),
                  pl.BlockSpec((None, 2, SSM_STATE, SSM_UB), blk)],
        out_specs=[pl.BlockSpec((None, 2, SUBLANES, LANES), blk),
                   pl.BlockSpec((None, SSM_UB, 2 * SSM_CB), lambda i: (i, 0, 0)),
                   pl.BlockSpec((None, 2 * SSM_CB, SSM_UB), lambda i: (i, 0, 0))],
        out_shape=[jax.ShapeDtypeStruct((nblk, 2, SUBLANES, LANES), F32),
                   jax.ShapeDtypeStruct((nblk, SSM_UB, 2 * SSM_CB), BF16),
                   jax.ShapeDtypeStruct((nblk, 2 * SSM_CB, SSM_UB), BF16)],
        compiler_params=_params("parallel"),
        name="ssm_prep",
    )(arow, avec, bt, ct)


def _ssm_kernel(u_ref, bw_ref, cw_ref, abar_ref, d_ref, o_ref, s_ref, st_ref, *, tt, nb, ng):
    pitch = tt + SSM_PAD
    half = SSM_NJ // 2
    chains = [(b, gl) for b in range(nb) for gl in range(ng)]
    t_idx = pl.program_id(1)

    @pl.when(t_idx == 0)
    def _():
        st_ref[...] = jnp.zeros(st_ref.shape, F32)

    for q, (b, gl) in enumerate(chains):
        ucols = slice(gl * SSM_UB, (gl + 1) * SSM_UB)
        bu = jnp.dot(u_ref[b, :, ucols].astype(BF16), bw_ref[gl], preferred_element_type=F32)
        for j in range(SSM_NJ):
            s_ref[q, j * pitch:j * pitch + tt, :] = bu[:, j * LANES:(j + 1) * LANES]

    ar = [abar_ref[gl, 0] for gl in range(ng)]
    ai = [abar_ref[gl, 1] for gl in range(ng)]

    for gl in range(ng):
        qs = [b * ng + gl for b in range(nb)]
        state = [(st_ref[2 * q], st_ref[2 * q + 1]) for q in qs]
        for t in range(tt):
            for idx, q in enumerate(qs):
                xr, xi = state[idx]
                sq = s_ref.at[q]
                br = sq[pl.ds(t, half, stride=pitch), :]
                bi = sq[pl.ds(half * pitch + t, half, stride=pitch), :]
                nr = ar[gl] * xr - ai[gl] * xi + br
                ni = ar[gl] * xi + ai[gl] * xr + bi
                sq[pl.ds(t, half, stride=pitch), :] = nr
                sq[pl.ds(half * pitch + t, half, stride=pitch), :] = ni
                state[idx] = (nr, ni)
        for idx, q in enumerate(qs):
            st_ref[2 * q] = state[idx][0]
            st_ref[2 * q + 1] = state[idx][1]

    for q, (b, gl) in enumerate(chains):
        ucols = slice(gl * SSM_UB, (gl + 1) * SSM_UB)
        xs = jnp.concatenate([s_ref[q, j * pitch:j * pitch + tt, :].astype(BF16) for j in range(SSM_NJ)], axis=1)
        acc = jnp.dot(xs, cw_ref[gl], preferred_element_type=F32)
        y = acc + d_ref[:, ucols] * u_ref[b, :, ucols]
        o_ref[b, :, ucols] = jax.nn.gelu(y).astype(o_ref.dtype)


def ssm_branch(proj3, bw, cw, abar, d_skip, li, tt, ng):
    bsz, seq, _ = proj3.shape
    ngb = SSM_WIDTH // SSM_UB
    uw = ng * SSM_UB
    col0 = 2 * CONV_WIDTH // uw
    blk0 = li * ngb // ng
    pitch = tt + SSM_PAD
    return pl.pallas_call(
        functools.partial(_ssm_kernel, tt=tt, nb=bsz, ng=ng),
        grid=(ngb // ng, seq // tt),
        in_specs=[pl.BlockSpec((bsz, tt, uw), lambda g, t: (0, t, col0 + g)),
                  pl.BlockSpec((ng, SSM_UB, 2 * SSM_CB), lambda g, t: (blk0 + g, 0, 0)),
                  pl.BlockSpec((ng, 2 * SSM_CB, SSM_UB), lambda g, t: (blk0 + g, 0, 0)),
                  pl.BlockSpec((ng, 2, SUBLANES, LANES), lambda g, t: (blk0 + g, 0, 0, 0)),
                  pl.BlockSpec((None, 1, uw), lambda g, t: (li, 0, g))],
        out_specs=pl.BlockSpec((bsz, tt, uw), lambda g, t: (0, t, g)),
        out_shape=jax.ShapeDtypeStruct((bsz, seq, SSM_WIDTH), BF16),
        scratch_shapes=[pltpu.VMEM((bsz * ng, SSM_NJ * pitch, LANES), F32),
                        pltpu.VMEM((2 * bsz * ng, SUBLANES, LANES), F32)],
        compiler_params=_params("parallel", "arbitrary"),
        name="ssm_branch",
    )(proj3, bw, cw, abar, d_skip)


def _zmix_kernel(hc_ref, ys_ref, wpw_ref, wga_ref, wgb_ref, la_ref, lb_ref, z_ref, pwb_ref, gab_ref, gbb_ref):
    @pl.when(pl.program_id(1) == 0)
    def _():
        pwb_ref[...] = wpw_ref[...].astype(BF16)
        gab_ref[...] = wga_ref[...].astype(BF16)
        gbb_ref[...] = wgb_ref[...].astype(BF16)

    hc = hc_ref[...]
    ys = ys_ref[...]
    for c in range(z_ref.shape[1] // MXU_COLS):
        sl = slice(c * MXU_COLS, (c + 1) * MXU_COLS)
        ya = jnp.dot(hc, pwb_ref[:, sl], preferred_element_type=F32)
        ga = jnp.dot(ys, gab_ref[:, sl], preferred_element_type=F32)
        gb = jnp.dot(ys, gbb_ref[:, sl], preferred_element_type=F32)
        yb = ga * _sigmoid(gb)
        z = _sigmoid(la_ref[:, sl]) * ya + _sigmoid(lb_ref[:, sl]) * yb
        z_ref[:, sl] = z.astype(z_ref.dtype)


def zmix(hc, ys, w_pw, w_glu, li, proj, tm, tn):
    m = hc.shape[0]
    nj = D_MODEL // tn
    gate0 = (2 * CONV_WIDTH + SSM_WIDTH) // tn
    return pl.pallas_call(
        _zmix_kernel,
        grid=(nj, m // tm),
        in_specs=[pl.BlockSpec((tm, CONV_WIDTH), lambda j, i: (i, 0)),
                  pl.BlockSpec((tm, SSM_WIDTH), lambda j, i: (i, 0)),
                  pl.BlockSpec((None, CONV_WIDTH, tn), lambda j, i: (li, 0, j)),
                  pl.BlockSpec((None, SSM_WIDTH, tn), lambda j, i: (li, 0, j)),
                  pl.BlockSpec((None, SSM_WIDTH, tn), lambda j, i: (li, 0, nj + j)),
                  pl.BlockSpec((tm, tn), lambda j, i: (i, gate0 + j)),
                  pl.BlockSpec((tm, tn), lambda j, i: (i, gate0 + nj + j))],
        out_specs=pl.BlockSpec((tm, tn), lambda j, i: (i, j)),
        out_shape=jax.ShapeDtypeStruct((m, D_MODEL), BF16),
        scratch_shapes=[pltpu.VMEM((CONV_WIDTH, tn), BF16),
                        pltpu.VMEM((SSM_WIDTH, tn), BF16),
                        pltpu.VMEM((SSM_WIDTH, tn), BF16)],
        compiler_params=_params("parallel", "arbitrary"),
        name="zmix",
    )(hc, ys, w_pw, w_glu, w_glu, proj, proj)


def _attn_kernel(q_ref, k_ref, v_ref, o_ref):
    scale = XA_HEAD_DIM ** -0.5
    for h in range(XA_HEADS):
        sl = slice(h * XA_HEAD_DIM, (h + 1) * XA_HEAD_DIM)
        s = lax.dot_general(q_ref[:, sl], k_ref[:, sl], (((1,), (1,)), ((), ())),
                            preferred_element_type=F32) * scale
        s = s - jnp.max(s, axis=-1, keepdims=True)
        p = jnp.exp(s)
        p = p / jnp.sum(p, axis=-1, keepdims=True)
        o = jnp.dot(p.astype(BF16), v_ref[:, sl], preferred_element_type=F32)
        o_ref[:, sl] = o.astype(o_ref.dtype)


def attention(q, kv, bsz, seq, tm):
    nt = seq // tm
    return pl.pallas_call(
        _attn_kernel,
        grid=(bsz, nt),
        in_specs=[pl.BlockSpec((tm, D_MODEL), lambda b, t: (b * nt + t, 0)),
                  pl.BlockSpec((MEM_LEN, D_MODEL), lambda b, t: (b, 0)),
                  pl.BlockSpec((MEM_LEN, D_MODEL), lambda b, t: (b, 1))],
        out_specs=pl.BlockSpec((tm, D_MODEL), lambda b, t: (b * nt + t, 0)),
        out_shape=jax.ShapeDtypeStruct((bsz * seq, D_MODEL), BF16),
        compiler_params=_params("parallel", "arbitrary"),
        name="attention",
    )(q, kv, kv)


FFN_HALO = 8
FFN_RB = 32


def _ffn_up_kernel(h_ref, wg_ref, wv_ref, dg_ref, dv_ref, o_ref, wgb_ref, wvb_ref,
                   dgh_ref, eg_ref, ev_ref, *, tm, tn, nt):
    nslab = tn // LANES
    s = pl.program_id(1)

    @pl.when(s == 0)
    def _():
        wgb_ref[...] = wg_ref[...].astype(BF16)
        wvb_ref[...] = wv_ref[...].astype(BF16)
        dgh_ref[...] = 0.5 * dg_ref[...]

    @pl.when(s % nt == 0)
    def _():
        eg_ref[:, 0:FFN_HALO, :] = jnp.zeros((nslab, FFN_HALO, LANES), F32)
        ev_ref[:, 0:FFN_HALO, :] = jnp.zeros((nslab, FFN_HALO, LANES), F32)

    h = h_ref[...]
    for e_ref, wb_ref in ((eg_ref, wgb_ref), (ev_ref, wvb_ref)):
        for c2 in range(tn // MXU_COLS):
            u = jnp.dot(h, wb_ref[:, c2 * MXU_COLS:(c2 + 1) * MXU_COLS], preferred_element_type=F32)
            for half in range(MXU_COLS // LANES):
                c = c2 * (MXU_COLS // LANES) + half
                e_ref[c, FFN_HALO:FFN_HALO + tm, :] = u[:, half * LANES:(half + 1) * LANES]

    first = FFN_HALO - (FFN_K - 1)
    for c in range(nslab):
        sl = slice(c * LANES, (c + 1) * LANES)
        for rb in range(tm // FFN_RB):
            r0 = rb * FFN_RB
            hg = dgh_ref[0:1, sl] * eg_ref[c, first + r0:first + r0 + FFN_RB, :]
            v = dv_ref[0:1, sl] * ev_ref[c, first + r0:first + r0 + FFN_RB, :]
            for k in range(1, FFN_K):
                hg = hg + dgh_ref[k:k + 1, sl] * eg_ref[c, first + r0 + k:first + r0 + k + FFN_RB, :]
                v = v + dv_ref[k:k + 1, sl] * ev_ref[c, first + r0 + k:first + r0 + k + FFN_RB, :]
            o_ref[r0:r0 + FFN_RB, sl] = ((hg * jnp.tanh(hg) + hg) * v).astype(o_ref.dtype)

    eg_ref[:, 0:FFN_HALO, :] = eg_ref[:, tm:tm + FFN_HALO, :]
    ev_ref[:, 0:FFN_HALO, :] = ev_ref[:, tm:tm + FFN_HALO, :]


def ffn_up(h, w_up, dw_w, li, bsz, seq, tm, tn):
    nt = seq // tm
    nj = D_FF // tn
    slabs = (tn // LANES, FFN_HALO + tm, LANES)
    return pl.pallas_call(
        functools.partial(_ffn_up_kernel, tm=tm, tn=tn, nt=nt),
        grid=(nj, bsz * nt),
        in_specs=[pl.BlockSpec((tm, D_MODEL), lambda j, s: (s, 0)),
                  pl.BlockSpec((None, D_MODEL, tn), lambda j, s: (li, 0, j)),
                  pl.BlockSpec((None, D_MODEL, tn), lambda j, s: (li, 0, nj + j)),
                  pl.BlockSpec((None, FFN_K, tn), lambda j, s: (li, 0, j)),
                  pl.BlockSpec((None, FFN_K, tn), lambda j, s: (li, 0, nj + j))],
        out_specs=pl.BlockSpec((tm, tn), lambda j, s: (s, j)),
        out_shape=jax.ShapeDtypeStruct((bsz * seq, D_FF), BF16),
        scratch_shapes=[pltpu.VMEM((D_MODEL, tn), BF16),
                        pltpu.VMEM((D_MODEL, tn), BF16),
                        pltpu.VMEM((FFN_K, tn), F32),
                        pltpu.VMEM(slabs, F32),
                        pltpu.VMEM(slabs, F32)],
        compiler_params=_params("parallel", "arbitrary"),
        name="ffn_up",
    )(h, w_up, w_up, dw_w, dw_w)


def kernel(x, mem, mix_norm_g, w_in, conv_dw_w, conv_dw_b, conv_ln_g, conv_ln_b, conv_w_pw,
           ssm_a_re, ssm_a_im, ssm_log_dt, ssm_b_re, ssm_b_im, ssm_c_re, ssm_c_im, ssm_d,
           ssm_w_glu, w_out, xa_norm_g, mem_norm_g, xa_w_q, xa_w_kv, xa_w_o,
           ffn_norm_g, ffn_w_up, ffn_dw_w, ffn_w_down, final_norm_g):
    bsz, seq, d = x.shape
    depth = w_in.shape[0]
    rows = bsz * seq
    xf = x.reshape(rows, d)
    memf = mem.reshape(bsz * MEM_LEN, d)

    w_down_b = ffn_w_down.astype(BF16)
    abar, bw, cw = ssm_prep(ssm_a_re, ssm_a_im, ssm_log_dt, ssm_b_re, ssm_b_im, ssm_c_re, ssm_c_im)
    d_skip = ssm_d.reshape(depth, 1, SSM_WIDTH)

    h = rmsnorm_rows(xf, mix_norm_g[0], 512, BF16)
    out = None
    for i in range(depth):
        proj = matmul_wcast(h, w_in, i, 1024, 1024, F32, "in_proj")
        hc = conv_branch(proj, conv_dw_w[i], conv_dw_b[i], conv_ln_g[i], conv_ln_b[i], bsz, seq, 512)
        ys = ssm_branch(proj.reshape(bsz, seq, -1), bw, cw, abar, d_skip, i, 512, 2).reshape(rows, SSM_WIDTH)
        z = zmix(hc, ys, conv_w_pw, ssm_w_glu, i, proj, 1024, 512)
        xf, h = matmul_res_norm(z, w_out, i, xf, xa_norm_g[i], 512, BF16, "out_proj")
        q = matmul_wcast(h, xa_w_q, i, 1024, 1024, BF16, "q_proj")
        mn = rmsnorm_rows(memf, mem_norm_g[i], 256, BF16)
        kv = matmul_wcast(mn, xa_w_kv, i, 512, 1024, BF16, "kv_proj")
        o = attention(q, kv, bsz, seq, 1024)
        xf, h = matmul_res_norm(o, xa_w_o, i, xf, ffn_norm_g[i], 512, BF16, "o_proj")
        act = ffn_up(h, ffn_w_up, ffn_dw_w, i, bsz, seq, 1024, 512)
        if i + 1 < depth:
            xf, h = matmul_cols_res_norm(act, w_down_b, i, xf, mix_norm_g[i + 1],
                                         512, 1024, 1, BF16, True, "down_proj")
        else:
            out = matmul_cols_res_norm(act, w_down_b, i, xf, final_norm_g,
                                       512, 1024, 1, F32, False, "down_proj_final")[0]
    return out.reshape(bsz, seq, d)
```

```python
import functools
import math

import jax
import jax.numpy as jnp
from jax import lax
from jax.experimental import pallas as pl
from jax.experimental.pallas import tpu as pltpu

D_MODEL = 2048
MEM_LEN = 256
CONV_WIDTH = 1024
CONV_K = 31
SSM_WIDTH = 1024
SSM_GROUP = 16
SSM_GROUPS = SSM_WIDTH // SSM_GROUP
SSM_STATE = 64
XA_HEADS = 4
XA_HEAD_DIM = D_MODEL // XA_HEADS
D_FF = 5632
FFN_K = 3
EPS = 1e-6

LANES = 128
SUBLANES = 8
MXU_COLS = 256
VMEM_LIMIT = 56 * 1024 * 1024

BF16 = jnp.bfloat16
F32 = jnp.float32


def _params(*sem):
    return pltpu.CompilerParams(dimension_semantics=sem, vmem_limit_bytes=VMEM_LIMIT)


def _rms(x, g):
    return x * lax.rsqrt(jnp.mean(x * x, axis=-1, keepdims=True) + EPS) * g


def _sigmoid(x):
    return 0.5 * jnp.tanh(0.5 * x) + 0.5


def _silu(x):
    hx = 0.5 * x
    return hx * jnp.tanh(hx) + hx


def _rmsnorm_kernel(x_ref, g_ref, o_ref):
    o_ref[...] = _rms(x_ref[...], g_ref[...]).astype(o_ref.dtype)


def rmsnorm_rows(x, g, tm, out_dtype):
    m, d = x.shape
    return pl.pallas_call(
        _rmsnorm_kernel,
        grid=(m // tm,),
        in_specs=[pl.BlockSpec((tm, d), lambda i: (i, 0)),
                  pl.BlockSpec((1, d), lambda i: (0, 0))],
        out_specs=pl.BlockSpec((tm, d), lambda i: (i, 0)),
        out_shape=jax.ShapeDtypeStruct((m, d), out_dtype),
        compiler_params=_params("parallel"),
        name="rmsnorm",
    )(x, g.reshape(1, d))


def _mm_kernel(a_ref, w_ref, o_ref):
    o_ref[...] = jnp.dot(a_ref[...], w_ref[...], preferred_element_type=F32).astype(o_ref.dtype)


def matmul(a, w, li, tm, tn, out_dtype, name):
    m, k = a.shape
    n = w.shape[2]
    return pl.pallas_call(
        _mm_kernel,
        grid=(m // tm, n // tn),
        in_specs=[pl.BlockSpec((tm, k), lambda i, j: (i, 0)),
                  pl.BlockSpec((None, k, tn), lambda i, j: (li, 0, j))],
        out_specs=pl.BlockSpec((tm, tn), lambda i, j: (i, j)),
        out_shape=jax.ShapeDtypeStruct((m, n), out_dtype),
        compiler_params=_params("parallel", "arbitrary"),
        name=name,
    )(a, w)


def _mm_wcast_kernel(a_ref, w_ref, o_ref, wb_ref):
    @pl.when(pl.program_id(1) == 0)
    def _():
        wb_ref[...] = w_ref[...].astype(BF16)

    o_ref[...] = jnp.dot(a_ref[...], wb_ref[...], preferred_element_type=F32).astype(o_ref.dtype)


def matmul_wcast(a, w, li, tm, tn, out_dtype, name):
    m, k = a.shape
    n = w.shape[2]
    return pl.pallas_call(
        _mm_wcast_kernel,
        grid=(n // tn, m // tm),
        in_specs=[pl.BlockSpec((tm, k), lambda j, i: (i, 0)),
                  pl.BlockSpec((None, k, tn), lambda j, i: (li, 0, j))],
        out_specs=pl.BlockSpec((tm, tn), lambda j, i: (i, j)),
        out_shape=jax.ShapeDtypeStruct((m, n), out_dtype),
        scratch_shapes=[pltpu.VMEM((k, tn), BF16)],
        compiler_params=_params("parallel", "arbitrary"),
        name=name,
    )(a, w)


def _norm_mm_wcast_kernel(x_ref, g_ref, w_ref, o_ref):
    h = _rms(x_ref[...], g_ref[...]).astype(BF16)
    o_ref[...] = jnp.dot(h, w_ref[...].astype(BF16), preferred_element_type=F32).astype(o_ref.dtype)


def norm_matmul_wcast(x, g, w, li, tm, tn, out_dtype, name):
    m, k = x.shape
    n = w.shape[2]
    return pl.pallas_call(
        _norm_mm_wcast_kernel,
        grid=(n // tn, m // tm),
        in_specs=[pl.BlockSpec((tm, k), lambda j, i: (i, 0)),
                  pl.BlockSpec((1, k), lambda j, i: (0, 0)),
                  pl.BlockSpec((None, k, tn), lambda j, i: (li, 0, j))],
        out_specs=pl.BlockSpec((tm, tn), lambda j, i: (i, j)),
        out_shape=jax.ShapeDtypeStruct((m, n), out_dtype),
        compiler_params=_params("parallel", "arbitrary"),
        name=name,
    )(x, g.reshape(1, k), w)


def _mm_res_norm_kernel(a_ref, w_ref, x_ref, g_ref, xo_ref, h_ref, wb_ref):
    @pl.when(pl.program_id(0) == 0)
    def _():
        wb_ref[...] = w_ref[...].astype(BF16)

    xn = x_ref[...] + jnp.dot(a_ref[...], wb_ref[...], preferred_element_type=F32)
    xo_ref[...] = xn
    h_ref[...] = _rms(xn, g_ref[...]).astype(h_ref.dtype)


def matmul_res_norm(a, w, li, x, g, tm, h_dtype, name):
    m, kdim = a.shape
    n = w.shape[2]
    row = lambda i: (i, 0)
    return pl.pallas_call(
        _mm_res_norm_kernel,
        grid=(m // tm,),
        in_specs=[pl.BlockSpec((tm, kdim), row),
                  pl.BlockSpec((None, kdim, n), lambda i: (li, 0, 0), pipeline_mode=pl.Buffered(1)),
                  pl.BlockSpec((tm, n), row),
                  pl.BlockSpec((1, n), lambda i: (0, 0))],
        out_specs=[pl.BlockSpec((tm, n), row), pl.BlockSpec((tm, n), row)],
        out_shape=[jax.ShapeDtypeStruct((m, n), F32), jax.ShapeDtypeStruct((m, n), h_dtype)],
        scratch_shapes=[pltpu.VMEM((kdim, n), BF16)],
        compiler_params=_params("arbitrary"),
        name=name,
    )(a, w, x, g.reshape(1, n))


def _mm_cols_res_norm_kernel(a_ref, w_ref, x_ref, g_ref, *refs, nk, nj, tn, emit_x):
    if emit_x:
        xo_ref, h_ref, row_ref = refs
    else:
        h_ref, row_ref = refs
    k = pl.program_id(1)
    j = pl.program_id(2)

    def chunk(first, last):
        base = x_ref[...] if first else row_ref[j]
        val = base + jnp.dot(a_ref[...], w_ref[...], preferred_element_type=F32)
        row_ref[j] = val
        if last and emit_x:
            xo_ref[...] = val

    pl.when(k == 0)(functools.partial(chunk, True, nk == 1))
    if nk > 2:
        pl.when((k > 0) & (k < nk - 1))(functools.partial(chunk, False, False))
    if nk > 1:
        pl.when(k == nk - 1)(functools.partial(chunk, False, True))

    @pl.when((k == nk - 1) & (j == nj - 1))
    def _():
        ssq = jnp.zeros((row_ref.shape[1], 1), F32)
        for jj in range(nj):
            blk = row_ref[jj]
            ssq = ssq + jnp.sum(blk * blk, axis=-1, keepdims=True)
        inv = lax.rsqrt(ssq * (1.0 / (nj * tn)) + EPS)
        for jj in range(nj):
            sl = slice(jj * tn, (jj + 1) * tn)
            h_ref[:, sl] = (row_ref[jj] * inv * g_ref[:, sl]).astype(h_ref.dtype)


def matmul_cols_res_norm(a, w, li, x, g, tm, tn, nk, h_dtype, emit_x, name):
    m, kdim = a.shape
    n = w.shape[2]
    nj = n // tn
    tk = kdim // nk
    last = nk - 1
    out_specs = [pl.BlockSpec((tm, n), lambda i, k, j: (i, 0))]
    out_shape = [jax.ShapeDtypeStruct((m, n), h_dtype)]
    if emit_x:
        out_specs = [pl.BlockSpec((tm, tn), lambda i, k, j: (i, jnp.where(k == last, j, 0)))] + out_specs
        out_shape = [jax.ShapeDtypeStruct((m, n), F32)] + out_shape
    return pl.pallas_call(
        functools.partial(_mm_cols_res_norm_kernel, nk=nk, nj=nj, tn=tn, emit_x=emit_x),
        grid=(m // tm, nk, nj),
        in_specs=[pl.BlockSpec((tm, tk), lambda i, k, j: (i, k)),
                  pl.BlockSpec((None, tk, tn), lambda i, k, j: (li, k, j)),
                  pl.BlockSpec((tm, tn), lambda i, k, j: (i, jnp.where(k == 0, j, nj - 1))),
                  pl.BlockSpec((1, n), lambda i, k, j: (0, 0))],
        out_specs=out_specs,
        out_shape=out_shape,
        scratch_shapes=[pltpu.VMEM((nj, tm, tn), F32)],
        compiler_params=_params("parallel", "arbitrary", "arbitrary"),
        name=name,
    )(a, w, x, g.reshape(1, n))


CONV_HALO = 32
CONV_RB = 32


def _conv_kernel(a_ref, b_ref, w_ref, bias_ref, lng_ref, lnb_ref, o_ref, ext_ref, y_ref, *, tm):
    nslab = CONV_WIDTH // LANES
    t = pl.program_id(1)

    @pl.when(t == 0)
    def _():
        ext_ref[:, 0:CONV_HALO, :] = jnp.zeros((nslab, CONV_HALO, LANES), F32)

    for c in range(nslab):
        sl = slice(c * LANES, (c + 1) * LANES)
        ext_ref[c, CONV_HALO:CONV_HALO + tm, :] = a_ref[:, sl] * _sigmoid(b_ref[:, sl])

    first = CONV_HALO - (CONV_K - 1)
    for c in range(nslab):
        sl = slice(c * LANES, (c + 1) * LANES)

        def row_block(rb, carry, c=c, sl=sl):
            r0 = pl.multiple_of(rb * CONV_RB, CONV_RB)
            acc = [jnp.broadcast_to(bias_ref[:, sl], (CONV_RB, LANES)), None]
            for k in range(CONV_K):
                term = w_ref[k:k + 1, sl] * ext_ref[c, pl.ds(r0 + first + k, CONV_RB), :]
                acc[k % 2] = term if acc[k % 2] is None else acc[k % 2] + term
            y_ref[pl.ds(r0, CONV_RB), sl] = acc[0] + acc[1]
            return carry

        lax.fori_loop(0, tm // CONV_RB, row_block, 0, unroll=8)

    ext_ref[:, 0:CONV_HALO, :] = ext_ref[:, tm:tm + CONV_HALO, :]

    def ln_block(rb, carry):
        r0 = pl.multiple_of(rb * CONV_RB, CONV_RB)
        y = y_ref[pl.ds(r0, CONV_RB), :]
        mu = jnp.mean(y, axis=-1, keepdims=True)
        yc = y - mu
        yn = yc * lax.rsqrt(jnp.mean(yc * yc, axis=-1, keepdims=True) + EPS)
        yn = yn * lng_ref[...] + lnb_ref[...]
        o_ref[pl.ds(r0, CONV_RB), :] = _silu(yn).astype(o_ref.dtype)
        return carry

    lax.fori_loop(0, tm // CONV_RB, ln_block, 0, unroll=16)


def conv_branch(proj, dw_w, dw_b, ln_g, ln_b, bsz, seq, tm):
    nt = seq // tm
    cw = CONV_WIDTH
    row = lambda b, t: (b * nt + t, 0)
    vec = lambda b, t: (0, 0)
    return pl.pallas_call(
        functools.partial(_conv_kernel, tm=tm),
        grid=(bsz, nt),
        in_specs=[pl.BlockSpec((tm, cw), lambda b, t: (b * nt + t, 0)),
                  pl.BlockSpec((tm, cw), lambda b, t: (b * nt + t, 1)),
                  pl.BlockSpec((CONV_K, cw), vec),
                  pl.BlockSpec((1, cw), vec),
                  pl.BlockSpec((1, cw), vec),
                  pl.BlockSpec((1, cw), vec)],
        out_specs=pl.BlockSpec((tm, cw), row),
        out_shape=jax.ShapeDtypeStruct((bsz * seq, cw), BF16),
        scratch_shapes=[pltpu.VMEM((cw // LANES, CONV_HALO + tm, LANES), F32),
                        pltpu.VMEM((tm, cw), F32)],
        compiler_params=_params("parallel", "arbitrary"),
        name="conv_branch",
    )(proj, proj, dw_w, dw_b.reshape(1, cw), ln_g.reshape(1, cw), ln_b.reshape(1, cw))


SSM_GB = 16
SSM_UB = SSM_GB * SSM_GROUP
SSM_CB = SSM_GB * SSM_STATE
SSM_NJ = 2 * SSM_CB // LANES
SSM_PAD = 4


def _zoh(ar, ai, ldt):
    ar = jnp.minimum(ar, -1e-4)
    dt = jnp.exp(ldt)
    mag = jnp.exp(dt * ar)
    return mag * jnp.cos(dt * ai), mag * jnp.sin(dt * ai), ar


def _ssm_prep_kernel(arow_ref, avec_ref, bt_ref, ct_ref, abar_ref, bw_ref, cw_ref):
    are, aim, _ = _zoh(avec_ref[0], avec_ref[1], avec_ref[2])
    abar_ref[0] = are
    abar_ref[1] = aim

    abar_re, abar_im, ar = _zoh(arow_ref[0], arow_ref[1], arow_ref[2])
    ai = arow_ref[1]
    den = ar * ar + ai * ai
    nr = abar_re - 1.0
    z_re = (nr * ar + abar_im * ai) / den
    z_im = (abar_im * ar - nr * ai) / den
    br = bt_ref[0]
    bi = bt_ref[1]
    bbar = (z_re * br - z_im * bi, z_re * bi + z_im * br)

    rg = lax.broadcasted_iota(jnp.int32, (SSM_UB, SSM_CB), 0) // SSM_GROUP
    cg = lax.broadcasted_iota(jnp.int32, (SSM_UB, SSM_CB), 1) // SSM_STATE
    for part in range(2):
        tiled = jnp.concatenate([bbar[part]] * SSM_GB, axis=0)
        bw_ref[:, part * SSM_CB:(part + 1) * SSM_CB] = jnp.where(rg == cg, tiled, 0.0).astype(BF16)

    rg = lax.broadcasted_iota(jnp.int32, (SSM_CB, SSM_UB), 0) // SSM_STATE
    cg = lax.broadcasted_iota(jnp.int32, (SSM_CB, SSM_UB), 1) // SSM_GROUP
    for part, sign in ((0, 1.0), (1, -1.0)):
        tiled = jnp.concatenate([sign * ct_ref[part]] * SSM_GB, axis=0)
        cw_ref[part * SSM_CB:(part + 1) * SSM_CB, :] = jnp.where(rg == cg, tiled, 0.0).astype(BF16)


def ssm_prep(a_re, a_im, log_dt, b_re, b_im, c_re, c_im):
    nl = a_re.shape[0]
    ngb = SSM_GROUPS // SSM_GB
    nblk = nl * ngb
    ldt = jnp.broadcast_to(log_dt[:, :, None], a_re.shape)
    a3 = jnp.stack([a_re, a_im, ldt], axis=1).reshape(nl, 3, ngb, SSM_CB)
    a3 = jnp.transpose(a3, (0, 2, 1, 3)).reshape(nblk, 3, SSM_CB)
    arow = a3.reshape(nblk, 3, 1, SSM_CB)
    avec = a3.reshape(nblk, 3, SUBLANES, LANES)

    def b_t(b):
        v = b.reshape(nl, ngb, SSM_CB, SSM_GROUP)
        return jnp.transpose(v, (0, 1, 3, 2)).reshape(nblk, SSM_GROUP, SSM_CB)

    def c_t(c):
        v = c.reshape(nl, ngb, SSM_GB, SSM_GROUP, SSM_STATE)
        return jnp.transpose(v, (0, 1, 4, 2, 3)).reshape(nblk, SSM_STATE, SSM_UB)

    bt = jnp.stack([b_t(b_re), b_t(b_im)], axis=1)
    ct = jnp.stack([c_t(c_re), c_t(c_im)], axis=1)
    blk = lambda i: (i, 0, 0, 0)
    return pl.pallas_call(
        _ssm_prep_kernel,
        grid=(nblk,),
        in_specs=[pl.BlockSpec((None, 3, 1, SSM_CB), blk),
                  pl.BlockSpec((None, 3, SUBLANES, LANES), blk),
                  pl.BlockSpec((None, 2, SSM_GROUP, SSM_CB), blk),
                  pl.BlockSpec((None, 2, SSM_STATE, SSM_UB), blk)],
        out_specs=[pl.BlockSpec((None, 2, SUBLANES, LANES), blk),
                   pl.BlockSpec((None, SSM_UB, 2 * SSM_CB), lambda i: (i, 0, 0)),
                   pl.BlockSpec((None, 2 * SSM_CB, SSM_UB), lambda i: (i, 0, 0))],
        out_shape=[jax.ShapeDtypeStruct((nblk, 2, SUBLANES, LANES), F32),
                   jax.ShapeDtypeStruct((nblk, SSM_UB, 2 * SSM_CB), BF16),
                   jax.ShapeDtypeStruct((nblk, 2 * SSM_CB, SSM_UB), BF16)],
        compiler_params=_params("parallel"),
        name="ssm_prep",
    )(arow, avec, bt, ct)


def _ssm_kernel(u_ref, bw_ref, cw_ref, abar_ref, d_ref, o_ref, s_ref, st_ref, *, tt, nb, ng):
    pitch = tt + SSM_PAD
    half = SSM_NJ // 2
    chains = [(b, gl) for b in range(nb) for gl in range(ng)]
    t_idx = pl.program_id(1)

    @pl.when(t_idx == 0)
    def _():
        st_ref[...] = jnp.zeros(st_ref.shape, F32)

    for q, (b, gl) in enumerate(chains):
        ucols = slice(gl * SSM_UB, (gl + 1) * SSM_UB)
        bu = jnp.dot(u_ref[b, :, ucols].astype(BF16), bw_ref[gl], preferred_element_type=F32)
        for j in range(SSM_NJ):
            s_ref[q, j * pitch:j * pitch + tt, :] = bu[:, j * LANES:(j + 1) * LANES]

    ar = [abar_ref[gl, 0] for gl in range(ng)]
    ai = [abar_ref[gl, 1] for gl in range(ng)]

    for gl in range(ng):
        qs = [b * ng + gl for b in range(nb)]
        state = [(st_ref[2 * q], st_ref[2 * q + 1]) for q in qs]
        for t in range(tt):
            for idx, q in enumerate(qs):
                xr, xi = state[idx]
                sq = s_ref.at[q]
                br = sq[pl.ds(t, half, stride=pitch), :]
                bi = sq[pl.ds(half * pitch + t, half, stride=pitch), :]
                nr = ar[gl] * xr - ai[gl] * xi + br
                ni = ar[gl] * xi + ai[gl] * xr + bi
                sq[pl.ds(t, half, stride=pitch), :] = nr
                sq[pl.ds(half * pitch + t, half, stride=pitch), :] = ni
                state[idx] = (nr, ni)
        for idx, q in enumerate(qs):
            st_ref[2 * q] = state[idx][0]
            st_ref[2 * q + 1] = state[idx][1]

    for q, (b, gl) in enumerate(chains):
        ucols = slice(gl * SSM_UB, (gl + 1) * SSM_UB)
        xs = jnp.concatenate([s_ref[q, j * pitch:j * pitch + tt, :].astype(BF16) for j in range(SSM_NJ)], axis=1)
        acc = jnp.dot(xs, cw_ref[gl], preferred_element_type=F32)
        y = acc + d_ref[:, ucols] * u_ref[b, :, ucols]
        o_ref[b, :, ucols] = jax.nn.gelu(y).astype(o_ref.dtype)


def ssm_branch(proj3, bw, cw, abar, d_skip, li, tt, ng):
    bsz, seq, _ = proj3.shape
    ngb = SSM_WIDTH // SSM_UB
    uw = ng * SSM_UB
    col0 = 2 * CONV_WIDTH // uw
    blk0 = li * ngb // ng
    pitch = tt + SSM_PAD
    return pl.pallas_call(
        functools.partial(_ssm_kernel, tt=tt, nb=bsz, ng=ng),
        grid=(ngb // ng, seq // tt),
        in_specs=[pl.BlockSpec((bsz, tt, uw), lambda g, t: (0, t, col0 + g)),
                  pl.BlockSpec((ng, SSM_UB, 2 * SSM_CB), lambda g, t: (blk0 + g, 0, 0)),
                  pl.BlockSpec((ng, 2 * SSM_CB, SSM_UB), lambda g, t: (blk0 + g, 0, 0)),
                  pl.BlockSpec((ng, 2, SUBLANES, LANES), lambda g, t: (blk0 + g, 0, 0, 0)),
                  pl.BlockSpec((None, 1, uw), lambda g, t: (li, 0, g))],
        out_specs=pl.BlockSpec((bsz, tt, uw), lambda g, t: (0, t, g)),
        out_shape=jax.ShapeDtypeStruct((bsz, seq, SSM_WIDTH), BF16),
        scratch_shapes=[pltpu.VMEM((bsz * ng, SSM_NJ * pitch, LANES), F32),
                        pltpu.VMEM((2 * bsz * ng, SUBLANES, LANES), F32)],
        compiler_params=_params("parallel", "arbitrary"),
        name="ssm_branch",
    )(proj3, bw, cw, abar, d_skip)


def _zmix_kernel(hc_ref, ys_ref, wpw_ref, wga_ref, wgb_ref, la_ref, lb_ref, z_ref, pwb_ref, gab_ref, gbb_ref):
    @pl.when(pl.program_id(1) == 0)
    def _():
        pwb_ref[...] = wpw_ref[...].astype(BF16)
        gab_ref[...] = wga_ref[...].astype(BF16)
        gbb_ref[...] = wgb_ref[...].astype(BF16)

    hc = hc_ref[...]
    ys = ys_ref[...]
    for c in range(z_ref.shape[1] // MXU_COLS):
        sl = slice(c * MXU_COLS, (c + 1) * MXU_COLS)
        ya = jnp.dot(hc, pwb_ref[:, sl], preferred_element_type=F32)
        ga = jnp.dot(ys, gab_ref[:, sl], preferred_element_type=F32)
        gb = jnp.dot(ys, gbb_ref[:, sl], preferred_element_type=F32)
        yb = ga * _sigmoid(gb)
        z = _sigmoid(la_ref[:, sl]) * ya + _sigmoid(lb_ref[:, sl]) * yb
        z_ref[:, sl] = z.astype(z_ref.dtype)


def zmix(hc, ys, w_pw, w_glu, li, proj, tm, tn):
    m = hc.shape[0]
    nj = D_MODEL // tn
    gate0 = (2 * CONV_WIDTH + SSM_WIDTH) // tn
    return pl.pallas_call(
        _zmix_kernel,
        grid=(nj, m // tm),
        in_specs=[pl.BlockSpec((tm, CONV_WIDTH), lambda j, i: (i, 0)),
                  pl.BlockSpec((tm, SSM_WIDTH), lambda j, i: (i, 0)),
                  pl.BlockSpec((None, CONV_WIDTH, tn), lambda j, i: (li, 0, j)),
                  pl.BlockSpec((None, SSM_WIDTH, tn), lambda j, i: (li, 0, j)),
                  pl.BlockSpec((None, SSM_WIDTH, tn), lambda j, i: (li, 0, nj + j)),
                  pl.BlockSpec((tm, tn), lambda j, i: (i, gate0 + j)),
                  pl.BlockSpec((tm, tn), lambda j, i: (i, gate0 + nj + j))],
        out_specs=pl.BlockSpec((tm, tn), lambda j, i: (i, j)),
        out_shape=jax.ShapeDtypeStruct((m, D_MODEL), BF16),
        scratch_shapes=[pltpu.VMEM((CONV_WIDTH, tn), BF16),
                        pltpu.VMEM((SSM_WIDTH, tn), BF16),
                        pltpu.VMEM((SSM_WIDTH, tn), BF16)],
        compiler_params=_params("parallel", "arbitrary"),
        name="zmix",
    )(hc, ys, w_pw, w_glu, w_glu, proj, proj)


def _attn_kernel(q_ref, k_ref, v_ref, o_ref):
    scale = XA_HEAD_DIM ** -0.5
    for h in range(XA_HEADS):
        sl = slice(h * XA_HEAD_DIM, (h + 1) * XA_HEAD_DIM)
        s = lax.dot_general(q_ref[:, sl], k_ref[:, sl], (((1,), (1,)), ((), ())),
                            preferred_element_type=F32) * scale
        s = s - jnp.max(s, axis=-1, keepdims=True)
        p = jnp.exp(s)
        p = p / jnp.sum(p, axis=-1, keepdims=True)
        o = jnp.dot(p.astype(BF16), v_ref[:, sl], preferred_element_type=F32)
        o_ref[:, sl] = o.astype(o_ref.dtype)


def attention(q, kv, bsz, seq, tm):
    nt = seq // tm
    return pl.pallas_call(
        _attn_kernel,
        grid=(bsz, nt),
        in_specs=[pl.BlockSpec((tm, D_MODEL), lambda b, t: (b * nt + t, 0)),
                  pl.BlockSpec((MEM_LEN, D_MODEL), lambda b, t: (b, 0)),
                  pl.BlockSpec((MEM_LEN, D_MODEL), lambda b, t: (b, 1))],
        out_specs=pl.BlockSpec((tm, D_MODEL), lambda b, t: (b * nt + t, 0)),
        out_shape=jax.ShapeDtypeStruct((bsz * seq, D_MODEL), BF16),
        compiler_params=_params("parallel", "arbitrary"),
        name="attention",
    )(q, kv, kv)


FFN_HALO = 8
FFN_RB = 32


def _ffn_up_kernel(h_ref, wg_ref, wv_ref, dg_ref, dv_ref, o_ref, wgb_ref, wvb_ref,
                   dgh_ref, eg_ref, ev_ref, *, tm, tn, nt):
    nslab = tn // LANES
    s = pl.program_id(1)

    @pl.when(s == 0)
    def _():
        wgb_ref[...] = wg_ref[...].astype(BF16)
        wvb_ref[...] = wv_ref[...].astype(BF16)
        dgh_ref[...] = 0.5 * dg_ref[...]

    @pl.when(s % nt == 0)
    def _():
        eg_ref[:, 0:FFN_HALO, :] = jnp.zeros((nslab, FFN_HALO, LANES), F32)
        ev_ref[:, 0:FFN_HALO, :] = jnp.zeros((nslab, FFN_HALO, LANES), F32)

    h = h_ref[...]
    for e_ref, wb_ref in ((eg_ref, wgb_ref), (ev_ref, wvb_ref)):
        for c2 in range(tn // MXU_COLS):
            u = jnp.dot(h, wb_ref[:, c2 * MXU_COLS:(c2 + 1) * MXU_COLS], preferred_element_type=F32)
            for half in range(MXU_COLS // LANES):
                c = c2 * (MXU_COLS // LANES) + half
                e_ref[c, FFN_HALO:FFN_HALO + tm, :] = u[:, half * LANES:(half + 1) * LANES]

    first = FFN_HALO - (FFN_K - 1)
    for c in range(nslab):
        sl = slice(c * LANES, (c + 1) * LANES)
        for rb in range(tm // FFN_RB):
            r0 = rb * FFN_RB
            hg = dgh_ref[0:1, sl] * eg_ref[c, first + r0:first + r0 + FFN_RB, :]
            v = dv_ref[0:1, sl] * ev_ref[c, first + r0:first + r0 + FFN_RB, :]
            for k in range(1, FFN_K):
                hg = hg + dgh_ref[k:k + 1, sl] * eg_ref[c, first + r0 + k:first + r0 + k + FFN_RB, :]
                v = v + dv_ref[k:k + 1, sl] * ev_ref[c, first + r0 + k:first + r0 + k + FFN_RB, :]
            o_ref[r0:r0 + FFN_RB, sl] = ((hg * jnp.tanh(hg) + hg) * v).astype(o_ref.dtype)

    eg_ref[:, 0:FFN_HALO, :] = eg_ref[:, tm:tm + FFN_HALO, :]
    ev_ref[:, 0:FFN_HALO, :] = ev_ref[:, tm:tm + FFN_HALO, :]


def ffn_up(h, w_up, dw_w, li, bsz, seq, tm, tn):
    nt = seq // tm
    nj = D_FF // tn
    slabs = (tn // LANES, FFN_HALO + tm, LANES)
    return pl.pallas_call(
        functools.partial(_ffn_up_kernel, tm=tm, tn=tn, nt=nt),
        grid=(nj, bsz * nt),
        in_specs=[pl.BlockSpec((tm, D_MODEL), lambda j, s: (s, 0)),
                  pl.BlockSpec((None, D_MODEL, tn), lambda j, s: (li, 0, j)),
                  pl.BlockSpec((None, D_MODEL, tn), lambda j, s: (li, 0, nj + j)),
                  pl.BlockSpec((None, FFN_K, tn), lambda j, s: (li, 0, j)),
                  pl.BlockSpec((None, FFN_K, tn), lambda j, s: (li, 0, nj + j))],
        out_specs=pl.BlockSpec((tm, tn), lambda j, s: (s, j)),
        out_shape=jax.ShapeDtypeStruct((bsz * seq, D_FF), BF16),
        scratch_shapes=[pltpu.VMEM((D_MODEL, tn), BF16),
                        pltpu.VMEM((D_MODEL, tn), BF16),
                        pltpu.VMEM((FFN_K, tn), F32),
                        pltpu.VMEM(slabs, F32),
                        pltpu.VMEM(slabs, F32)],
        compiler_params=_params("parallel", "arbitrary"),
        name="ffn_up",
    )(h, w_up, w_up, dw_w, dw_w)


def kernel(x, mem, mix_norm_g, w_in, conv_dw_w, conv_dw_b, conv_ln_g, conv_ln_b, conv_w_pw,
           ssm_a_re, ssm_a_im, ssm_log_dt, ssm_b_re, ssm_b_im, ssm_c_re, ssm_c_im, ssm_d,
           ssm_w_glu, w_out, xa_norm_g, mem_norm_g, xa_w_q, xa_w_kv, xa_w_o,
           ffn_norm_g, ffn_w_up, ffn_dw_w, ffn_w_down, final_norm_g):
    bsz, seq, d = x.shape
    depth = w_in.shape[0]
    rows = bsz * seq
    xf = x.reshape(rows, d)
    memf = mem.reshape(bsz * MEM_LEN, d)

    w_down_b = ffn_w_down.astype(BF16)
    abar, bw, cw = ssm_prep(ssm_a_re, ssm_a_im, ssm_log_dt, ssm_b_re, ssm_b_im, ssm_c_re, ssm_c_im)
    d_skip = ssm_d.reshape(depth, 1, SSM_WIDTH)

    h = rmsnorm_rows(xf, mix_norm_g[0], 512, BF16)
    out = None
    for i in range(depth):
        proj = matmul_wcast(h, w_in, i, 1024, 1024, F32, "in_proj")
        hc = conv_branch(proj, conv_dw_w[i], conv_dw_b[i], conv_ln_g[i], conv_ln_b[i], bsz, seq, 512)
        ys = ssm_branch(proj.reshape(bsz, seq, -1), bw, cw, abar, d_skip, i, 512, 2).reshape(rows, SSM_WIDTH)
        z = zmix(hc, ys, conv_w_pw, ssm_w_glu, i, proj, 1024, 512)
        xf, h = matmul_res_norm(z, w_out, i, xf, xa_norm_g[i], 512, BF16, "out_proj")
        q = matmul_wcast(h, xa_w_q, i, 1024, 1024, BF16, "q_proj")
        kv = norm_matmul_wcast(memf, mem_norm_g[i], xa_w_kv, i, 512, 1024, BF16, "kv_proj")
        o = attention(q, kv, bsz, seq, 1024)
        xf, h = matmul_res_norm(o, xa_w_o, i, xf, ffn_norm_g[i], 512, BF16, "o_proj")
        act = ffn_up(h, ffn_w_up, ffn_dw_w, i, bsz, seq, 1024, 512)
        if i + 1 < depth:
            xf, h = matmul_cols_res_norm(act, w_down_b, i, xf, mix_norm_g[i + 1],
                                         512, 1024, 1, BF16, True, "down_proj")
        else:
            out = matmul_cols_res_norm(act, w_down_b, i, xf, final_norm_g,
                                       512, 1024, 1, F32, False, "down_proj_final")[0]
    return out.reshape(bsz, seq, d)
```
